```python
import math
import jax
import jax.numpy as jnp
from jax import lax
import numpy as np

D_MODEL = 2048
BATCH = 4
SEQ = 4096
DEPTH = 2
DEC_BATCH = 128
DEC_SEQ = 8
PAST_LEN = 16384
PAGE_SIZE = 128

N_A_LAYERS = DEPTH // 2
N_B_LAYERS = DEPTH - N_A_LAYERS
D_FF = 5632
SSM_EXPAND = 2
D_INNER = SSM_EXPAND * D_MODEL
SSM_HEAD_DIM = 64
SSM_HEADS = D_INNER // SSM_HEAD_DIM
SSM_GROUPS = 8
D_STATE = 128
CONV_W = 4
CONV_DIM = D_INNER + 2 * SSM_GROUPS * D_STATE
IN_PROJ_DIM = 2 * D_INNER + 2 * SSM_GROUPS * D_STATE + SSM_HEADS
SSD_CHUNK = 128
MLA_HEADS = 16
Q_LORA_RANK = 768
KV_LORA_RANK = 512
QK_NOPE_DIM = 128
QK_ROPE_DIM = 64
V_HEAD_DIM = 128
ROPE_THETA = 10000.0
SOFTMAX_SCALE = (QK_NOPE_DIM + QK_ROPE_DIM) ** -0.5
ATTN_Q_BLOCK = 128
RMS_EPS = 1e-6

kernel_name = 'hybrid_ssd_mla_yoco_macaron_step'

f32 = jnp.float32


def rms_norm(x, g):
    xf = x.astype(f32)
    y = xf * lax.rsqrt(jnp.mean(xf * xf, axis=-1, keepdims=True) + RMS_EPS)
    return (y * g.astype(f32)).astype(x.dtype)


def swiglu(x, w_gate, w_up, w_down):
    return (jax.nn.silu(x @ w_gate) * (x @ w_up)) @ w_down


def apply_rope(x, pos):
    half = x.shape[-1] // 2
    inv = ROPE_THETA ** (-jnp.arange(half, dtype=f32) / half)
    ang = pos[:, None] * inv[None, :]
    cos = jnp.cos(ang)[None, :, None, :]
    sin = jnp.sin(ang)[None, :, None, :]
    x1 = x[..., :half].astype(f32)
    x2 = x[..., half:].astype(f32)
    return jnp.concatenate([x1 * cos - x2 * sin, x1 * sin + x2 * cos], axis=-1).astype(x.dtype)


def ssd_scan(x, a, Bm, Cm, h0, chunk):
    bt, L, H, P = x.shape
    G, N = Bm.shape[2], Bm.shape[3]
    E = H // G
    nc = L // chunk
    x = x.reshape(bt, nc, chunk, G, E, P)
    a = a.astype(f32).reshape(bt, nc, chunk, G, E)
    Bm = Bm.reshape(bt, nc, chunk, G, N)
    Cm = Cm.reshape(bt, nc, chunk, G, N)
    acs = jnp.cumsum(a, axis=2)
    causal = jnp.tril(jnp.ones((chunk, chunk), dtype=bool))[:, :, None, None]
    decay = jnp.exp(jnp.where(causal, acs[:, :, :, None] - acs[:, :, None, :], -jnp.inf))
    cb = jnp.einsum('bclgn,bcsgn->bclsg', Cm, Bm)
    y_diag = jnp.einsum('bclsge,bcsgep->bclgep', cb[..., None] * decay, x)
    xd = x * jnp.exp(acs[:, :, -1:] - acs)[..., None]
    st = jnp.einsum('bclgn,bclgep->bcgepn', Bm, xd)
    chunk_decay = jnp.exp(acs[:, :, -1])

    def step(h, inp):
        dec, s = inp
        return dec[..., None, None] * h + s, h

    h_last, h_in = lax.scan(step, h0.reshape(bt, G, E, P, N).astype(st.dtype),
                            (jnp.moveaxis(chunk_decay, 1, 0), jnp.moveaxis(st, 1, 0)))
    h_in = jnp.moveaxis(h_in, 0, 1)
    y_off = jnp.einsum('bclgn,bcgepn->bclgep', Cm, h_in) * jnp.exp(acs)[..., None]
    y = (y_diag + y_off).reshape(bt, L, H, P)
    return y, h_last.reshape(bt, H, P, N)


def gated_group_rms_norm(y, z, g):
    bt, L, C = y.shape
    v = (y * jax.nn.silu(z)).astype(f32).reshape(bt, L, SSM_GROUPS, C // SSM_GROUPS)
    v = v * lax.rsqrt(jnp.mean(v * v, axis=-1, keepdims=True) + RMS_EPS)
    return (v.reshape(bt, L, C) * g.astype(f32)).astype(y.dtype)


def mamba_mixer(u, conv_prev, ssm_prev, w_in, conv_w, conv_b, dt_bias, A_log, D_skip, norm_g, w_out, chunk):
    bt, L, _ = u.shape
    proj = u @ w_in
    z = proj[..., :D_INNER]
    xbc = proj[..., D_INNER:D_INNER + CONV_DIM]
    dt_raw = proj[..., D_INNER + CONV_DIM:]
    xp = jnp.concatenate([conv_prev.astype(xbc.dtype), xbc], axis=1)
    conv = conv_b + xp[:, 0:L] * conv_w[0]
    for k in range(1, CONV_W):
        conv = conv + xp[:, k:k + L] * conv_w[k]
    xbc = jax.nn.silu(conv)
    new_conv = xp[:, L:]
    gn = SSM_GROUPS * D_STATE
    xs = xbc[..., :D_INNER].reshape(bt, L, SSM_HEADS, SSM_HEAD_DIM)
    Bm = xbc[..., D_INNER:D_INNER + gn].reshape(bt, L, SSM_GROUPS, D_STATE)
    Cm = xbc[..., D_INNER + gn:].reshape(bt, L, SSM_GROUPS, D_STATE)
    dt = jax.nn.softplus((dt_raw + dt_bias).astype(f32))
    A = -jnp.exp(A_log.astype(f32))
    y, new_ssm = ssd_scan(xs * dt[..., None], dt * A, Bm, Cm, ssm_prev, chunk)
    y = y + xs * D_skip[:, None]
    y = y.reshape(bt, L, D_INNER).astype(u.dtype)
    y = gated_group_rms_norm(y, z, norm_g)
    return y @ w_out, new_conv, new_ssm.astype(u.dtype)


def shared_kv(h, kv_norm_g, w_dkv, kv_latent_norm_g, pos):
    hn = rms_norm(h, kv_norm_g)
    ckr = hn @ w_dkv
    c = rms_norm(ckr[..., :KV_LORA_RANK], kv_latent_norm_g)
    kr = apply_rope(ckr[..., KV_LORA_RANK:][:, :, None, :], pos)[:, :, 0]
    return c, kr


def mla_query(u, w_dq, q_norm_g, w_uq, pos):
    cq = rms_norm(u @ w_dq, q_norm_g)
    q = jnp.einsum('bsq,qhd->bshd', cq, w_uq)
    return q[..., :QK_NOPE_DIM], apply_rope(q[..., QK_NOPE_DIM:], pos)


def attend_prompt(q_nope, q_rope, c, kr, w_uk, w_uv):
    bt, L, H, _ = q_nope.shape
    k_nope = jnp.einsum('bsc,chd->bshd', c, w_uk)
    v = jnp.einsum('bsc,chv->bshv', c, w_uv)
    outs = []
    for i in range(L // ATTN_Q_BLOCK):
        q0 = i * ATTN_Q_BLOCK
        kend = q0 + ATTN_Q_BLOCK
        s = (jnp.einsum('bqhd,bkhd->bhqk', q_nope[:, q0:kend], k_nope[:, :kend])
             + jnp.einsum('bqhr,bkr->bhqk', q_rope[:, q0:kend], kr[:, :kend])).astype(f32) * SOFTMAX_SCALE
        mask = (q0 + jnp.arange(ATTN_Q_BLOCK))[:, None] >= jnp.arange(kend)[None, :]
        p = jax.nn.softmax(jnp.where(mask, s, -jnp.inf), axis=-1).astype(v.dtype)
        outs.append(jnp.einsum('bhqk,bkhv->bqhv', p, v[:, :kend]))
    return jnp.concatenate(outs, axis=1)


def attend_sample(q_nope, q_rope, c_new, kr_new, cache_lat, cache_kr, page_table, w_uk, w_uv):
    S = q_nope.shape[1]
    q_lat = jnp.einsum('bshd,chd->bshc', q_nope, w_uk).astype(f32)
    q_r = q_rope.astype(f32)
    s_new = (jnp.einsum('bshc,bkc->bshk', q_lat, c_new.astype(f32))
             + jnp.einsum('bshr,bkr->bshk', q_r, kr_new.astype(f32))) * SOFTMAX_SCALE
    causal = jnp.tril(jnp.ones((S, S), dtype=bool))[None, :, None, :]
    s_new = jnp.where(causal, s_new, -jnp.inf)
    m0 = jnp.max(s_new, axis=-1)
    p0 = jnp.exp(s_new - m0[..., None])
    l0 = jnp.sum(p0, axis=-1)
    acc0 = jnp.einsum('bshk,bkc->bshc', p0, c_new.astype(f32))

    def step(carry, phys):
        m, l, acc = carry
        cb = cache_lat[phys].astype(f32)
        rb = cache_kr[phys].astype(f32)
        s = (jnp.einsum('bshc,bkc->bshk', q_lat, cb) + jnp.einsum('bshr,bkr->bshk', q_r, rb)) * SOFTMAX_SCALE
        m_new = jnp.maximum(m, jnp.max(s, axis=-1))
        alpha = jnp.exp(m - m_new)
        p = jnp.exp(s - m_new[..., None])
        l = l * alpha + jnp.sum(p, axis=-1)
        acc = acc * alpha[..., None] + jnp.einsum('bshk,bkc->bshc', p, cb)
        return (m_new, l, acc), None

    (m, l, acc), _ = lax.scan(step, (m0, l0, acc0), page_table.T)
    o_lat = (acc / l[..., None]).astype(q_nope.dtype)
    return jnp.einsum('bshc,chv->bshv', o_lat, w_uv)


def setup_inputs(seed: int = 0) -> dict:
    key = jax.random.key(seed)
    ks = iter(jax.random.split(key, 48))
    n_pages = PAST_LEN // PAGE_SIZE
    n_pool = (DEC_BATCH * n_pages * 5) // 4

    def nrm(shape, scale):
        return jax.random.normal(next(ks), shape, f32) * scale

    def gain(shape):
        return 1.0 + 0.05 * jax.random.normal(next(ks), shape, f32)

    x_prompt = nrm((BATCH, SEQ, D_MODEL), 1.0)
    x_sample = nrm((DEC_BATCH, DEC_SEQ, D_MODEL), 1.0)
    cache_kv_latent = nrm((n_pool, PAGE_SIZE, KV_LORA_RANK), 1.0)
    cache_k_rope = nrm((n_pool, PAGE_SIZE, QK_ROPE_DIM), 1.0)
    state_ssm = nrm((N_A_LAYERS, DEC_BATCH, SSM_HEADS, SSM_HEAD_DIM, D_STATE), 0.5)
    state_conv = nrm((N_A_LAYERS, DEC_BATCH, CONV_W - 1, CONV_DIM), 1.0)
    perm = jax.random.permutation(next(ks), n_pool)[: DEC_BATCH * n_pages]
    page_table = perm.reshape(DEC_BATCH, n_pages).astype(jnp.int32)
    ffn_pre_g = gain((DEPTH, 2, D_MODEL))
    ffn_post_g = gain((DEPTH, 2, D_MODEL))
    ffn_w_gate = nrm((DEPTH, 2, D_MODEL, D_FF), D_MODEL ** -0.5)
    ffn_w_up = nrm((DEPTH, 2, D_MODEL, D_FF), D_MODEL ** -0.5)
    ffn_w_down = nrm((DEPTH, 2, D_FF, D_MODEL), D_FF ** -0.5)
    mix_pre_g = gain((DEPTH, D_MODEL))
    mix_post_g = gain((DEPTH, D_MODEL))
    m_in_proj = nrm((N_A_LAYERS, D_MODEL, IN_PROJ_DIM), D_MODEL ** -0.5)
    m_conv_w = nrm((N_A_LAYERS, CONV_W, CONV_DIM), CONV_W ** -0.5)
    m_conv_b = nrm((N_A_LAYERS, CONV_DIM), 0.02)
    dt0 = jnp.exp(jax.random.uniform(next(ks), (N_A_LAYERS, SSM_HEADS), f32, math.log(1e-3), math.log(1e-1)))
    m_dt_bias = dt0 + jnp.log(-jnp.expm1(-dt0))
    m_A_log = jnp.log(jax.random.uniform(next(ks), (N_A_LAYERS, SSM_HEADS), f32, 1.0, 16.0))
    m_D = 1.0 + 0.1 * jax.random.normal(next(ks), (N_A_LAYERS, SSM_HEADS), f32)
    m_norm_g = gain((N_A_LAYERS, D_INNER))
    m_out_proj = nrm((N_A_LAYERS, D_INNER, D_MODEL), D_INNER ** -0.5)
    kv_norm_g = gain((D_MODEL,))
    w_dkv = nrm((D_MODEL, KV_LORA_RANK + QK_ROPE_DIM), D_MODEL ** -0.5)
    kv_latent_norm_g = gain((KV_LORA_RANK,))
    w_uk = nrm((KV_LORA_RANK, MLA_HEADS, QK_NOPE_DIM), KV_LORA_RANK ** -0.5)
    w_uv = nrm((KV_LORA_RANK, MLA_HEADS, V_HEAD_DIM), KV_LORA_RANK ** -0.5)
    q_w_dq = nrm((N_B_LAYERS, D_MODEL, Q_LORA_RANK), D_MODEL ** -0.5)
    q_norm_g = gain((N_B_LAYERS, Q_LORA_RANK))
    q_w_uq = nrm((N_B_LAYERS, Q_LORA_RANK, MLA_HEADS, QK_NOPE_DIM + QK_ROPE_DIM), Q_LORA_RANK ** -0.5)
    attn_w_o = nrm((N_B_LAYERS, MLA_HEADS, V_HEAD_DIM, D_MODEL), (MLA_HEADS * V_HEAD_DIM) ** -0.5)
    return {'x_prompt': x_prompt, 'x_sample': x_sample, 'cache_kv_latent': cache_kv_latent,
            'cache_k_rope': cache_k_rope, 'state_ssm': state_ssm, 'state_conv': state_conv,
            'page_table': page_table, 'ffn_pre_g': ffn_pre_g, 'ffn_post_g': ffn_post_g,
            'ffn_w_gate': ffn_w_gate, 'ffn_w_up': ffn_w_up, 'ffn_w_down': ffn_w_down,
            'mix_pre_g': mix_pre_g, 'mix_post_g': mix_post_g, 'm_in_proj': m_in_proj,
            'm_conv_w': m_conv_w, 'm_conv_b': m_conv_b, 'm_dt_bias': m_dt_bias, 'm_A_log': m_A_log,
            'm_D': m_D, 'm_norm_g': m_norm_g, 'm_out_proj': m_out_proj, 'kv_norm_g': kv_norm_g,
            'w_dkv': w_dkv, 'kv_latent_norm_g': kv_latent_norm_g, 'w_uk': w_uk, 'w_uv': w_uv,
            'q_w_dq': q_w_dq, 'q_norm_g': q_norm_g, 'q_w_uq': q_w_uq, 'attn_w_o': attn_w_o}


def reference(x_prompt, x_sample, cache_kv_latent, cache_k_rope, state_ssm, state_conv, page_table,
              ffn_pre_g, ffn_post_g, ffn_w_gate, ffn_w_up, ffn_w_down, mix_pre_g, mix_post_g,
              m_in_proj, m_conv_w, m_conv_b, m_dt_bias, m_A_log, m_D, m_norm_g, m_out_proj,
              kv_norm_g, w_dkv, kv_latent_norm_g, w_uk, w_uv, q_w_dq, q_norm_g, q_w_uq, attn_w_o):

    def half_ffn(h, layer, j):
        y = swiglu(rms_norm(h, ffn_pre_g[layer, j]), ffn_w_gate[layer, j], ffn_w_up[layer, j], ffn_w_down[layer, j])
        return h + 0.5 * rms_norm(y, ffn_post_g[layer, j])

    def run_trunk(h, pos, conv0, ssm0, chunk, attend):
        new_conv, new_ssm = [], []
        c_kv, k_r = None, None
        for layer in range(DEPTH):
            h = half_ffn(h, layer, 0)
            u = rms_norm(h, mix_pre_g[layer])
            if layer < N_A_LAYERS:
                out, cs, ss = mamba_mixer(u, conv0[layer], ssm0[layer], m_in_proj[layer], m_conv_w[layer],
                                          m_conv_b[layer], m_dt_bias[layer], m_A_log[layer], m_D[layer],
                                          m_norm_g[layer], m_out_proj[layer], chunk)
                new_conv.append(cs)
                new_ssm.append(ss)
            else:
                j = layer - N_A_LAYERS
                q_nope, q_rope = mla_query(u, q_w_dq[j], q_norm_g[j], q_w_uq[j], pos)
                o = attend(q_nope, q_rope, c_kv, k_r)
                out = jnp.einsum('bshv,hvd->bsd', o, attn_w_o[j])
            h = h + rms_norm(out, mix_post_g[layer])
            h = half_ffn(h, layer, 1)
            if layer == N_A_LAYERS - 1:
                c_kv, k_r = shared_kv(h, kv_norm_g, w_dkv, kv_latent_norm_g, pos)
        return h, c_kv, k_r, jnp.stack(new_conv), jnp.stack(new_ssm)

    bp, seq, _ = x_prompt.shape
    pos_p = jnp.arange(seq, dtype=f32)
    conv0_p = jnp.zeros((N_A_LAYERS, bp, CONV_W - 1, CONV_DIM), x_prompt.dtype)
    ssm0_p = jnp.zeros((N_A_LAYERS, bp, SSM_HEADS, SSM_HEAD_DIM, D_STATE), x_prompt.dtype)
    y_prompt, kv_latent_prompt, k_rope_prompt, conv_prompt, ssm_prompt = run_trunk(
        x_prompt, pos_p, conv0_p, ssm0_p, SSD_CHUNK,
        lambda qn, qr, c, kr: attend_prompt(qn, qr, c, kr, w_uk, w_uv))

    dec_seq = x_sample.shape[1]
    past_len = page_table.shape[1] * PAGE_SIZE
    pos_s = past_len + jnp.arange(dec_seq, dtype=f32)
    y_sample, kv_latent_sample, k_rope_sample, conv_sample, ssm_sample = run_trunk(
        x_sample, pos_s, state_conv, state_ssm, dec_seq,
        lambda qn, qr, c, kr: attend_sample(qn, qr, c, kr, cache_kv_latent, cache_k_rope, page_table, w_uk, w_uv))

    return (y_prompt, y_sample, kv_latent_prompt, k_rope_prompt, ssm_prompt, conv_prompt,
            kv_latent_sample, k_rope_sample, ssm_sample, conv_sample)
```

```python
import functools

import jax
import jax.numpy as jnp
from jax import lax
from jax.experimental import pallas as pl
from jax.experimental.pallas import tpu as pltpu

f32 = jnp.float32
bf16 = jnp.bfloat16

RMS_EPS = 1e-6
ROPE_THETA = 10000.0
SSD_CHUNK = 128
SSM_GROUPS = 8
CONV_W = 4
LANES = 128
V7X_VMEM_BYTES = 64 * 1024 * 1024
VMEM_LIMIT = V7X_VMEM_BYTES - 8 * 1024 * 1024

_NT = (((1,), (1,)), ((), ()))
_TN = (((0,), (0,)), ((), ()))


def _rms(x, g):
    return x * lax.rsqrt(jnp.mean(x * x, axis=-1, keepdims=True) + RMS_EPS) * g


def _silu(x):
    return x * jax.nn.sigmoid(x)


def _softplus(x):
    return jnp.maximum(x, 0.0) + jnp.log1p(jnp.exp(-jnp.abs(x)))


def _params(*sem):
    return pltpu.CompilerParams(dimension_semantics=sem, vmem_limit_bytes=VMEM_LIMIT)


def _ffn_kernel(h_ref, gpre_ref, wg_ref, wu_ref, wd_ref, gpost_ref, o_ref, xn_ref, acc_ref, *, nj):
    j = pl.program_id(1)

    @pl.when(j == 0)
    def _():
        xn_ref[...] = _rms(h_ref[...], gpre_ref[...]).astype(bf16)

    xn = xn_ref[...]
    g = jnp.dot(xn, wg_ref[...], preferred_element_type=f32)
    u = jnp.dot(xn, wu_ref[...], preferred_element_type=f32)
    part = jnp.dot((_silu(g) * u).astype(bf16), wd_ref[...], preferred_element_type=f32)

    @pl.when(j == 0)
    def _():
        acc_ref[...] = part

    @pl.when(j > 0)
    def _():
        acc_ref[...] += part

    @pl.when(j == nj - 1)
    def _():
        o_ref[...] = h_ref[...] + 0.5 * _rms(acc_ref[...], gpost_ref[...])


def ffn_half(h, g_pre, wg, wu, wd, g_post, *, tm, tf):
    T, D = h.shape
    F = wg.shape[1]
    nj = F // tf
    return pl.pallas_call(
        functools.partial(_ffn_kernel, nj=nj),
        grid=(T // tm, nj),
        in_specs=[
            pl.BlockSpec((tm, D), lambda i, j: (i, 0)),
            pl.BlockSpec((1, D), lambda i, j: (0, 0)),
            pl.BlockSpec((D, tf), lambda i, j: (0, j)),
            pl.BlockSpec((D, tf), lambda i, j: (0, j)),
            pl.BlockSpec((tf, D), lambda i, j: (j, 0)),
            pl.BlockSpec((1, D), lambda i, j: (0, 0)),
        ],
        out_specs=pl.BlockSpec((tm, D), lambda i, j: (i, 0)),
        out_shape=jax.ShapeDtypeStruct((T, D), f32),
        scratch_shapes=[pltpu.VMEM((tm, D), bf16), pltpu.VMEM((tm, D), f32)],
        compiler_params=_params("parallel", "arbitrary"),
        name="ffn_half",
    )(h, g_pre, wg, wu, wd, g_post)


def _mm_kernel(*refs, prenorm, dual, na, mode, nj, tn, scale, use_xn, n_lat):
    refs = list(refs)
    xa_ref = refs.pop(0)
    xb_ref = refs.pop(0) if dual else None
    gpre_ref = refs.pop(0) if prenorm else None
    w_ref = refs.pop(0)
    i = pl.program_id(0)
    j = pl.program_id(1)

    def load_x(x_ref):
        x = x_ref[...]
        if prenorm:
            x = _rms(x.astype(f32), gpre_ref[...])
        return x.astype(bf16)

    if use_xn:
        xn_ref = refs[-1] if mode in ("plain", "kv", "rope") or nj == 1 else refs[-2]

        if dual:
            @pl.when((j == 0) & (i < na))
            def _():
                xn_ref[...] = load_x(xa_ref)

            @pl.when((j == 0) & (i >= na))
            def _():
                xn_ref[...] = load_x(xb_ref)
        else:
            @pl.when(j == 0)
            def _():
                xn_ref[...] = load_x(xa_ref)

        xn = xn_ref[...]
    else:
        xn = xa_ref[...]
    acc = jnp.dot(xn, w_ref[...], preferred_element_type=f32)

    if mode == "plain":
        o_ref = refs[0]
        o_ref[...] = (acc * scale if scale != 1.0 else acc).astype(o_ref.dtype)
    elif mode in ("norm", "resnorm"):
        if mode == "resnorm":
            res_ref, gpost_ref, o_ref = refs[0], refs[1], refs[2]
        else:
            res_ref, gpost_ref, o_ref = None, refs[0], refs[1]

        def finish(slabs):
            n = nj * tn
            ssq = sum(jnp.sum(s * s, axis=-1, keepdims=True) for s in slabs)
            inv = lax.rsqrt(ssq * (1.0 / n) + RMS_EPS)
            for k, s in enumerate(slabs):
                y = s * inv * gpost_ref[:, k * tn:(k + 1) * tn]
                if res_ref is not None:
                    y = res_ref[:, k * tn:(k + 1) * tn] + y
                o_ref[:, k * tn:(k + 1) * tn] = y.astype(o_ref.dtype)

        if nj == 1:
            finish([acc])
        else:
            slab_ref = refs[-1]
            slab_ref[j] = acc

            @pl.when(j == nj - 1)
            def _():
                finish([slab_ref[k] for k in range(nj)])
    elif mode == "kv":
        gpost_ref, cos_ref, sin_ref, lat_ref, kr_ref, krb_ref = refs[:6]
        lat_ref[...] = _rms(acc[:, :n_lat], gpost_ref[...])
        r = cos_ref.shape[1]
        kr = acc[:, n_lat:n_lat + r] * cos_ref[...] + acc[:, n_lat + r:n_lat + 2 * r] * sin_ref[...]
        kr_ref[...] = kr
        krb_ref[...] = kr.astype(bf16)
    elif mode == "rope":
        cos_ref, sin_ref, o_ref = refs[:3]
        half = acc.shape[1] // 2
        reps = half // cos_ref.shape[1]
        cos = jnp.concatenate([cos_ref[...]] * reps, axis=1)
        sin = jnp.concatenate([sin_ref[...]] * reps, axis=1)
        o_ref[...] = ((acc[:, :half] * cos + acc[:, half:] * sin) * scale).astype(o_ref.dtype)
    else:
        raise ValueError(mode)


def fused_matmul(x, w, *, tm, tn=None, mode="plain", g_pre=None, xb=None, res=None, g_post=None,
                 cos=None, sin=None, scale=1.0, out_dtype=f32, n_rows=None, row_block_offset=0,
                 n_lat=0, name="fused_matmul"):
    K, N = w.shape
    if tn is None:
        tn = N
    nj = N // tn
    dual = xb is not None
    na = x.shape[0] // tm
    M = (x.shape[0] + (xb.shape[0] if dual else 0)) if n_rows is None else n_rows
    ni = M // tm
    off = row_block_offset
    prenorm = g_pre is not None
    use_xn = prenorm or dual or x.dtype != bf16

    args, in_specs = [], []
    if dual:
        args += [x, xb]
        in_specs += [pl.BlockSpec((tm, K), lambda i, j: (jnp.minimum(i, na - 1), 0)),
                     pl.BlockSpec((tm, K), lambda i, j: (jnp.maximum(i - na, 0), 0))]
    else:
        args.append(x)
        in_specs.append(pl.BlockSpec((tm, K), lambda i, j: (i + off, 0)))
    if prenorm:
        args.append(g_pre)
        in_specs.append(pl.BlockSpec((1, K), lambda i, j: (0, 0)))
    args.append(w)
    in_specs.append(pl.BlockSpec((K, tn), lambda i, j: (0, j)))

    scratch = []
    if mode == "plain":
        out_shape = jax.ShapeDtypeStruct((M, N), out_dtype)
        out_specs = pl.BlockSpec((tm, tn), lambda i, j: (i, j))
    elif mode in ("norm", "resnorm"):
        if mode == "resnorm":
            args.append(res)
            in_specs.append(pl.BlockSpec((tm, N), lambda i, j: (i + off, 0)))
        args.append(g_post)
        in_specs.append(pl.BlockSpec((1, N), lambda i, j: (0, 0)))
        out_shape = jax.ShapeDtypeStruct((M, N), out_dtype)
        out_specs = pl.BlockSpec((tm, N), lambda i, j: (i, 0))
    elif mode == "kv":
        assert nj == 1
        r = cos.shape[1]
        args += [g_post, cos, sin]
        in_specs += [pl.BlockSpec((1, n_lat), lambda i, j: (0, 0)),
                     pl.BlockSpec((tm, r), lambda i, j: (i + off, 0)),
                     pl.BlockSpec((tm, r), lambda i, j: (i + off, 0))]
        out_shape = (jax.ShapeDtypeStruct((M, n_lat), f32), jax.ShapeDtypeStruct((M, r), f32),
                     jax.ShapeDtypeStruct((M, r), bf16))
        out_specs = (pl.BlockSpec((tm, n_lat), lambda i, j: (i, 0)), pl.BlockSpec((tm, r), lambda i, j: (i, 0)),
                     pl.BlockSpec((tm, r), lambda i, j: (i, 0)))
    elif mode == "rope":
        assert nj == 1
        r = cos.shape[1]
        args += [cos, sin]
        in_specs += [pl.BlockSpec((tm, r), lambda i, j: (i + off, 0)),
                     pl.BlockSpec((tm, r), lambda i, j: (i + off, 0))]
        out_shape = jax.ShapeDtypeStruct((M, N // 2), out_dtype)
        out_specs = pl.BlockSpec((tm, N // 2), lambda i, j: (i, 0))
    else:
        raise ValueError(mode)
    if mode in ("norm", "resnorm") and nj > 1:
        if use_xn:
            scratch.append(pltpu.VMEM((tm, K), bf16))
        scratch.append(pltpu.VMEM((nj, tm, tn), f32))
    elif use_xn:
        scratch.append(pltpu.VMEM((tm, K), bf16))

    kern = functools.partial(_mm_kernel, prenorm=prenorm, dual=dual, na=na, mode=mode, nj=nj, tn=tn,
                             scale=scale, use_xn=use_xn, n_lat=n_lat)
    return pl.pallas_call(
        kern, grid=(ni, nj), in_specs=in_specs, out_specs=out_specs, out_shape=out_shape,
        scratch_shapes=scratch, compiler_params=_params("parallel", "arbitrary"), name=name,
    )(*args)


def _bmm_kernel(x_ref, w_ref, o_ref):
    o_ref[...] = jnp.dot(x_ref[...], w_ref[0], preferred_element_type=f32).astype(o_ref.dtype)


def head_matmul(x, w, *, out_dtype=bf16, name="head_matmul"):
    H, K, N = w.shape
    M = x.shape[0]
    return pl.pallas_call(
        _bmm_kernel, grid=(H,),
        in_specs=[pl.BlockSpec((M, K), lambda h: (0, h)), pl.BlockSpec((1, K, N), lambda h: (h, 0, 0))],
        out_specs=pl.BlockSpec((M, N), lambda h: (0, h)),
        out_shape=jax.ShapeDtypeStruct((M, H * N), out_dtype),
        compiler_params=_params("parallel"), name=name,
    )(x, w)


def _conv_silu(prev, cur, w_ref, b_ref, n_prev):
    L = cur.shape[0]
    xp = jnp.concatenate([prev, cur], axis=0)
    base = n_prev - (CONV_W - 1)
    acc = b_ref[...] + xp[base:base + L] * w_ref[0:1, :]
    for k in range(1, CONV_W):
        acc = acc + xp[base + k:base + k + L] * w_ref[k:k + 1, :]
    return _silu(acc)


def _gated_group_norm(y, z, g):
    v = y * _silu(z)
    return v * lax.rsqrt(jnp.mean(v * v, axis=-1, keepdims=True) + RMS_EPS) * g


def _ssd_prompt_kernel(z_ref, x_ref, bc_ref, dt_ref, cwx_ref, cwbc_ref, cbx_ref, cbbc_ref, dtb_ref, alog_ref,
                       dx_ref, ng_ref, yg_ref, st_ref, xprev_ref, bcprev_ref, *, L, H, N, G):
    c = pl.program_id(1)
    n_prev = xprev_ref.shape[0]
    gw = G * N
    hpg = H // G
    cpg = x_ref.shape[1] // G

    @pl.when(c == 0)
    def _():
        st_ref[...] = jnp.zeros_like(st_ref)
        xprev_ref[...] = jnp.zeros_like(xprev_ref)
        bcprev_ref[...] = jnp.zeros_like(bcprev_ref)

    xcur = x_ref[...]
    bccur = bc_ref[...]
    xc = _conv_silu(xprev_ref[...], xcur, cwx_ref, cbx_ref, n_prev)
    bcc = _conv_silu(bcprev_ref[...], bccur, cwbc_ref, cbbc_ref, n_prev)
    xprev_ref[...] = xcur[L - n_prev:]
    bcprev_ref[...] = bccur[L - n_prev:]

    dt = _softplus(dt_ref[...] + dtb_ref[...])
    a = dt * (-jnp.exp(alog_ref[...]))
    row = lax.broadcasted_iota(jnp.int32, (L, L), 0)
    col = lax.broadcasted_iota(jnp.int32, (L, L), 1)
    causal = row >= col
    acs = jnp.dot(causal.astype(f32), a, preferred_element_type=f32, precision=lax.Precision.HIGHEST)
    lane = lax.broadcasted_iota(jnp.int32, (L, LANES), 1)
    sub = lax.broadcasted_iota(jnp.int32, (LANES, LANES), 0)
    rows = jnp.where(lane < LANES // 2, acs, pltpu.roll(dt, LANES // 2, 1)).T
    acs_last = acs[L - 1:L, :]
    eacs = jnp.exp(acs)
    seg = jnp.exp(acs_last - acs) * dt
    dec = jnp.exp(acs_last)

    lo = lane < LANES // 2
    for g in range(G):
        Bg = bcc[:, g * N:(g + 1) * N]
        Cg = bcc[:, gw + g * N:gw + (g + 1) * N]
        cb = lax.dot_general(Cg.astype(bf16), Bg.astype(bf16), _NT, preferred_element_type=f32)
        ys = []
        for pp in range(hpg // 2):
            h0 = g * hpg + 2 * pp
            q = h0 // 2
            ms, cs, bw = [], [], []
            for e in (h0, h0 + 1):
                diff = acs[:, e:e + 1] - rows[e:e + 1, :]
                decay = jnp.exp(jnp.where(causal, diff, -jnp.inf))
                ms.append(cb * decay * rows[LANES // 2 + e:LANES // 2 + e + 1, :])
                cs.append(Cg * eacs[:, e:e + 1])
                bw.append(Bg * seg[:, e:e + 1])
            xpair = xc[:, q * LANES:(q + 1) * LANES]
            xbd = jnp.concatenate([jnp.where(lo, xpair, 0.0), jnp.where(lo, 0.0, xpair)], axis=0).astype(bf16)
            hp = st_ref[0, q]
            hbd = jnp.concatenate([jnp.where(sub < LANES // 2, hp, 0.0), jnp.where(sub < LANES // 2, 0.0, hp)],
                                  axis=1).astype(bf16)
            y = jnp.dot(jnp.concatenate(ms, axis=1).astype(bf16), xbd, preferred_element_type=f32)
            y = y + lax.dot_general(jnp.concatenate(cs, axis=1).astype(bf16), hbd, _NT, preferred_element_type=f32)
            y = y + xpair * dx_ref[:, q * LANES:(q + 1) * LANES]
            st = lax.dot_general(xbd, jnp.concatenate(bw, axis=0).astype(bf16), _TN, preferred_element_type=f32)
            decp = jnp.where(sub < LANES // 2, dec[:, h0:h0 + 1], dec[:, h0 + 1:h0 + 2])
            st_ref[0, q] = decp * hp + st
            ys.append(y)
        sl = slice(g * cpg, (g + 1) * cpg)
        yg_ref[:, sl] = _gated_group_norm(jnp.concatenate(ys, axis=1), z_ref[:, sl], ng_ref[:, sl]).astype(yg_ref.dtype)


def ssd_prompt(zx, dtp, cw, cb, dtb, alog, dx, ng, *, B, S, H, P, N, G):
    L = SSD_CHUNK
    nc = S // L
    di = H * P
    gw = 2 * G * N
    assert 2 * P == LANES and N == LANES and di % gw == 0 and H <= LANES // 2
    cwx, cwbc = cw[:, :di], cw[:, di:]
    cbx, cbbc = cb[:, :di], cb[:, di:]
    kern = functools.partial(_ssd_prompt_kernel, L=L, H=H, N=N, G=G)
    const = lambda b, c: (0, 0)
    return pl.pallas_call(
        kern, grid=(B, nc),
        in_specs=[
            pl.BlockSpec((L, di), lambda b, c: (b * nc + c, 0)),
            pl.BlockSpec((L, di), lambda b, c: (b * nc + c, 1)),
            pl.BlockSpec((L, gw), lambda b, c: (b * nc + c, 2 * di // gw)),
            pl.BlockSpec((L, LANES), lambda b, c: (b * nc + c, 0)),
            pl.BlockSpec((CONV_W, di), const), pl.BlockSpec((CONV_W, gw), const),
            pl.BlockSpec((1, di), const), pl.BlockSpec((1, gw), const),
            pl.BlockSpec((1, LANES), const), pl.BlockSpec((1, LANES), const),
            pl.BlockSpec((1, di), const), pl.BlockSpec((1, di), const),
        ],
        out_specs=(pl.BlockSpec((L, di), lambda b, c: (b * nc + c, 0)),
                   pl.BlockSpec((1, H // 2, 2 * P, N), lambda b, c: (b, 0, 0, 0))),
        out_shape=(jax.ShapeDtypeStruct((B * S, di), f32), jax.ShapeDtypeStruct((B, H // 2, 2 * P, N), f32)),
        scratch_shapes=[pltpu.VMEM((8, di), f32), pltpu.VMEM((8, gw), f32)],
        compiler_params=_params("parallel", "arbitrary"), name="ssd_prompt",
    )(zx, zx, zx, dtp, cwx, cwbc, cbx, cbbc, dtb, alog, dx, ng)


def _ssd_sample_kernel(z_ref, x_ref, bc_ref, dtx_ref, cpx_ref, cpbc_ref, h0_ref, cwx_ref, cwbc_ref, cbx_ref,
                       cbbc_ref, dtb_ref, alog_ref, dx_ref, ng_ref, yg_ref, st_ref, *, L, N, G):
    di = x_ref.shape[1]
    gw = G * N
    cpg = di // G
    xc = _conv_silu(cpx_ref[0], x_ref[...], cwx_ref, cbx_ref, CONV_W - 1)
    bcc = _conv_silu(cpbc_ref[0], bc_ref[...], cwbc_ref, cbbc_ref, CONV_W - 1)
    dt = _softplus(dtx_ref[...] + dtb_ref[...])
    a = dt * (-jnp.exp(alog_ref[...]))
    row = lax.broadcasted_iota(jnp.int32, (L, di), 0)
    acs = jnp.zeros((L, di), f32)
    for s in range(L):
        acs = acs + jnp.where(row >= s, jnp.broadcast_to(a[s:s + 1, :], (L, di)), 0.0)
    acs_last = acs[L - 1:L, :]
    xdt = xc * dt

    cbx = []
    yoff = []
    for g in range(G):
        Bg = bcc[:, g * N:(g + 1) * N].astype(bf16)
        Cg = bcc[:, gw + g * N:gw + (g + 1) * N].astype(bf16)
        cb = lax.dot_general(Cg, Bg, _NT, preferred_element_type=f32)
        cbx.append([jnp.broadcast_to(cb[:, s:s + 1], (L, cpg)) for s in range(L)])
        hg = h0_ref[0, g]
        yoff.append(lax.dot_general(Cg, hg.astype(bf16), _NT, preferred_element_type=f32))
        sl = slice(g * cpg, (g + 1) * cpg)
        xd = (xdt[:, sl] * jnp.exp(acs_last[:, sl] - acs[:, sl])).astype(bf16)
        st = lax.dot_general(xd, Bg, _TN, preferred_element_type=f32)
        dcol = jnp.exp(jnp.broadcast_to(acs_last[:, sl], (L, cpg)).T[:, 0:1])
        st_ref[0, g] = dcol * hg + st
    y = jnp.exp(acs) * jnp.concatenate(yoff, axis=1) + xc * dx_ref[...]
    for s in range(L):
        decay = jnp.exp(jnp.where(row >= s, acs - jnp.broadcast_to(acs[s:s + 1, :], (L, di)), -jnp.inf))
        cbs = jnp.concatenate([cbx[g][s] for g in range(G)], axis=1)
        y = y + cbs * decay * jnp.broadcast_to(xdt[s:s + 1, :], (L, di))
    for g in range(G):
        sl = slice(g * cpg, (g + 1) * cpg)
        yg_ref[:, sl] = _gated_group_norm(y[:, sl], z_ref[:, sl], ng_ref[:, sl]).astype(yg_ref.dtype)


def ssd_sample(zx, dtx, conv_prev, h0, cw, cb, dtbx, alogx, dx, ng, *, B, L, row0, G, N):
    di = dtx.shape[1]
    gw = 2 * G * N
    cpg = di // G
    cwx, cwbc = cw[:, :di], cw[:, di:]
    cbx, cbbc = cb[:, :di], cb[:, di:]
    rb = row0 // L
    kern = functools.partial(_ssd_sample_kernel, L=L, N=N, G=G)
    const = lambda b: (0, 0)
    return pl.pallas_call(
        kern, grid=(B,),
        in_specs=[
            pl.BlockSpec((L, di), lambda b: (rb + b, 0)),
            pl.BlockSpec((L, di), lambda b: (rb + b, 1)),
            pl.BlockSpec((L, gw), lambda b: (rb + b, 2 * di // gw)),
            pl.BlockSpec((L, di), lambda b: (b, 0)),
            pl.BlockSpec((1, CONV_W - 1, di), lambda b: (b, 0, 0)),
            pl.BlockSpec((1, CONV_W - 1, gw), lambda b: (b, 0, di // gw)),
            pl.BlockSpec((1, G, cpg, N), lambda b: (b, 0, 0, 0)),
            pl.BlockSpec((CONV_W, di), const), pl.BlockSpec((CONV_W, gw), const),
            pl.BlockSpec((1, di), const), pl.BlockSpec((1, gw), const),
            pl.BlockSpec((1, di), const), pl.BlockSpec((1, di), const),
            pl.BlockSpec((1, di), const), pl.BlockSpec((1, di), const),
        ],
        out_specs=(pl.BlockSpec((L, di), lambda b: (b, 0)),
                   pl.BlockSpec((1, G, cpg, N), lambda b: (b, 0, 0, 0))),
        out_shape=(jax.ShapeDtypeStruct((B * L, di), f32), jax.ShapeDtypeStruct(h0.shape, f32)),
        compiler_params=_params("parallel"), name="ssd_sample",
    )(zx, zx, zx, dtx, conv_prev, conv_prev, h0, cwx, cwbc, cbx, cbbc, dtbx, alogx, dx, ng)


def _attn_prompt_kernel(qn_ref, qr_ref, kn_ref, v_ref, kr_ref, o_ref, *, tq):
    qi = pl.program_id(2)
    qn = qn_ref[...]
    qr = qr_ref[0]
    dv = v_ref.shape[1]

    def scores(k0):
        ks = pl.ds(pl.multiple_of(k0, tq), tq)
        s = lax.dot_general(qn, kn_ref[ks, :], _NT, preferred_element_type=f32)
        return s + lax.dot_general(qr, kr_ref[ks, :], _NT, preferred_element_type=f32), ks

    def update(carry, s, ks):
        m, l, acc = carry
        m_new = jnp.maximum(m, jnp.max(s, axis=-1, keepdims=True))
        alpha = jnp.exp(m - m_new)
        p = jnp.exp(s - m_new)
        l = l * alpha + jnp.sum(p, axis=-1, keepdims=True)
        acc = acc * alpha + jnp.dot(p.astype(bf16), v_ref[ks, :], preferred_element_type=f32)
        return m_new, l, acc

    def body(kb, carry):
        s, ks = scores(kb * tq)
        return update(carry, s, ks)

    init = (jnp.full((tq, 1), -jnp.inf, f32), jnp.zeros((tq, 1), f32), jnp.zeros((tq, dv), f32))
    carry = lax.fori_loop(0, qi, body, init)
    s, ks = scores(qi * tq)
    r = lax.broadcasted_iota(jnp.int32, (tq, tq), 0)
    c = lax.broadcasted_iota(jnp.int32, (tq, tq), 1)
    _, l, acc = update(carry, jnp.where(r >= c, s, -jnp.inf), ks)
    o_ref[...] = (acc / l).astype(o_ref.dtype)


def attn_prompt(qn, qr_hm, kv, krb, *, B, S, H, dn, dv, tq):
    nq = S // tq
    dr = krb.shape[1]
    assert dn == dv
    return pl.pallas_call(
        functools.partial(_attn_prompt_kernel, tq=tq), grid=(B, H, nq),
        in_specs=[
            pl.BlockSpec((tq, dn), lambda b, h, i: (b * nq + i, h)),
            pl.BlockSpec((1, tq, dr), lambda b, h, i: (h, b * nq + i, 0)),
            pl.BlockSpec((S, dn), lambda b, h, i: (b, h)),
            pl.BlockSpec((S, dv), lambda b, h, i: (b, H + h)),
            pl.BlockSpec((S, dr), lambda b, h, i: (b, 0)),
        ],
        out_specs=pl.BlockSpec((tq, dv), lambda b, h, i: (b * nq + i, h)),
        out_shape=jax.ShapeDtypeStruct((B * S, H * dv), bf16),
        compiler_params=_params("parallel", "parallel", "arbitrary"), name="attn_prompt",
    )(qn, qr_hm, kv, kv, krb)


def _attn_sample_kernel(pt_ref, *refs, npg, P, H):
    del pt_ref
    ql_ref, qr_ref, cn_ref, krn_ref = refs[:4]
    lat_refs = refs[4:4 + P]
    kr_refs = refs[4 + P:4 + 2 * P]
    o_ref, m_ref, l_ref, acc_ref = refs[4 + 2 * P:]
    j = pl.program_id(1)
    ql = ql_ref[0]
    qr = qr_ref[0]
    R = ql.shape[0]

    @pl.when(j == 0)
    def _():
        cn = cn_ref[0].astype(bf16)
        S = cn.shape[0]
        s = (lax.dot_general(ql, cn, _NT, preferred_element_type=f32)
             + lax.dot_general(qr, krn_ref[0].astype(bf16), _NT, preferred_element_type=f32))
        qpos = lax.broadcasted_iota(jnp.int32, (R, S), 0) // H
        kpos = lax.broadcasted_iota(jnp.int32, (R, S), 1)
        s = jnp.where(kpos <= qpos, s, -jnp.inf)
        m = jnp.max(s, axis=-1, keepdims=True)
        p = jnp.exp(s - m)
        m_ref[...] = m
        l_ref[...] = jnp.sum(p, axis=-1, keepdims=True)
        acc_ref[...] = jnp.dot(p.astype(bf16), cn, preferred_element_type=f32)

    cbs = [r[0].astype(bf16) for r in lat_refs]
    s = jnp.concatenate(
        [lax.dot_general(ql, cb, _NT, preferred_element_type=f32)
         + lax.dot_general(qr, kr[0].astype(bf16), _NT, preferred_element_type=f32)
         for cb, kr in zip(cbs, kr_refs)], axis=1)
    m = m_ref[...]
    m_new = jnp.maximum(m, jnp.max(s, axis=-1, keepdims=True))
    alpha = jnp.exp(m - m_new)
    p = jnp.exp(s - m_new)
    m_ref[...] = m_new
    l_ref[...] = l_ref[...] * alpha + jnp.sum(p, axis=-1, keepdims=True)
    pg = cbs[0].shape[0]
    pv = sum(jnp.dot(p[:, k * pg:(k + 1) * pg].astype(bf16), cbs[k], preferred_element_type=f32) for k in range(P))
    acc_ref[...] = acc_ref[...] * alpha + pv

    @pl.when(j == npg - 1)
    def _():
        o_ref[0] = (acc_ref[...] / l_ref[...]).astype(o_ref.dtype)


def attn_sample(page_table, ql, qr, c_new, kr_new, cache_lat, cache_kr, *, H, pages_per_step):
    B, R, C = ql.shape
    dr = qr.shape[2]
    S = c_new.shape[1]
    n_pages = page_table.shape[1]
    P = pages_per_step
    npg = n_pages // P
    pg = cache_lat.shape[1]
    pt = page_table.reshape(-1)

    def page_map(k):
        return lambda b, j, pt_ref: (pt_ref[b * n_pages + j * P + k], 0, 0)

    seq = lambda b, j, pt_ref: (b, 0, 0)
    in_specs = [pl.BlockSpec((1, R, C), seq), pl.BlockSpec((1, R, dr), seq),
                pl.BlockSpec((1, S, C), seq), pl.BlockSpec((1, S, dr), seq)]
    in_specs += [pl.BlockSpec((1, pg, C), page_map(k)) for k in range(P)]
    in_specs += [pl.BlockSpec((1, pg, dr), page_map(k)) for k in range(P)]
    return pl.pallas_call(
        functools.partial(_attn_sample_kernel, npg=npg, P=P, H=H),
        grid_spec=pltpu.PrefetchScalarGridSpec(
            num_scalar_prefetch=1, grid=(B, npg), in_specs=in_specs,
            out_specs=pl.BlockSpec((1, R, C), seq),
            scratch_shapes=[pltpu.VMEM((R, 1), f32), pltpu.VMEM((R, 1), f32), pltpu.VMEM((R, C), f32)]),
        out_shape=jax.ShapeDtypeStruct((B, R, C), bf16),
        compiler_params=_params("parallel", "arbitrary"), name="attn_sample",
    )(pt, ql, qr, c_new, kr_new, *([cache_lat] * P), *([cache_kr] * P))


def _rope_tables(pos, half):
    inv = ROPE_THETA ** (-jnp.arange(half, dtype=f32) / half)
    ang = pos[:, None] * inv[None, :]
    cos, sin = jnp.cos(ang), jnp.sin(ang)
    return jnp.concatenate([cos, cos], axis=1), jnp.concatenate([-sin, sin], axis=1)


def _swap_halves(w, r):
    k, n = w.shape
    return w.reshape(k, n // r, 2, r // 2)[:, :, ::-1, :].reshape(k, n)


def kernel(x_prompt, x_sample, cache_kv_latent, cache_k_rope, state_ssm, state_conv, page_table, ffn_pre_g, ffn_post_g, ffn_w_gate, ffn_w_up, ffn_w_down, mix_pre_g, mix_post_g, m_in_proj, m_conv_w, m_conv_b, m_dt_bias, m_A_log, m_D, m_norm_g, m_out_proj, kv_norm_g, w_dkv, kv_latent_norm_g, w_uk, w_uv, q_w_dq, q_norm_g, q_w_uq, attn_w_o):
    Bp, S, D = x_prompt.shape
    Bs, Ls, _ = x_sample.shape
    Tp, Ts = Bp * S, Bs * Ls
    T = Tp + Ts
    depth = ffn_pre_g.shape[0]
    n_a = state_ssm.shape[0]
    assert depth == 2 and n_a == 1, "layer pattern: one SSD layer then one MLA layer"
    H, P, N = state_ssm.shape[2:]
    G = SSM_GROUPS
    di = H * P
    conv_dim = state_conv.shape[3]
    C = cache_kv_latent.shape[2]
    dr = cache_k_rope.shape[2]
    Hm, dn = w_uk.shape[1:]
    dv = w_uv.shape[2]
    past_len = page_table.shape[1] * cache_kv_latent.shape[1]
    scale = float(dn + dr) ** -0.5
    tm = 512
    row = lambda v: v.reshape(1, -1)

    def ffn(h, layer, j):
        return ffn_half(h, row(ffn_pre_g[layer, j]), ffn_w_gate[layer, j].astype(bf16),
                        ffn_w_up[layer, j].astype(bf16), ffn_w_down[layer, j].astype(bf16),
                        row(ffn_post_g[layer, j]), tm=tm, tf=512)

    pos = jnp.concatenate([jnp.tile(jnp.arange(S, dtype=f32), Bp),
                           jnp.tile(past_len + jnp.arange(Ls, dtype=f32), Bs)])
    cos64, sin64 = _rope_tables(pos, dr // 2)
    cos128, sin128 = jnp.tile(cos64, (1, 2)), jnp.tile(sin64, (1, 2))

    h = jnp.concatenate([x_prompt.reshape(Tp, D), x_sample.reshape(Ts, D)], axis=0)

    h = ffn(h, 0, 0)
    w_in = m_in_proj[0]
    nzx = di + conv_dim
    w_zx = w_in[:, :nzx].astype(bf16)
    w_dt = w_in[:, nzx:]
    w_dtp = jnp.pad(w_dt, ((0, 0), (0, LANES - H))).astype(bf16)
    w_dtx = jnp.repeat(w_dt, P, axis=1).astype(bf16)
    g_mix0 = row(mix_pre_g[0])
    zx = fused_matmul(h, w_zx, tm=tm, tn=1024, g_pre=g_mix0, name="in_proj_zx")
    dtp = fused_matmul(h, w_dtp, tm=tm, g_pre=g_mix0, n_rows=Tp, name="in_proj_dt")
    dtx = fused_matmul(h, w_dtx, tm=tm, tn=1024, g_pre=g_mix0, n_rows=Ts, row_block_offset=Tp // tm,
                       name="in_proj_dtx")
    cw, cb = m_conv_w[0], row(m_conv_b[0])
    dx = row(jnp.repeat(m_D[0], P))
    ng = row(m_norm_g[0])
    pad_h = lambda v: row(jnp.pad(v, (0, LANES - H)))
    yg_p, ssm_p = ssd_prompt(zx, dtp, cw, cb, pad_h(m_dt_bias[0]), pad_h(m_A_log[0]), dx, ng,
                             B=Bp, S=S, H=H, P=P, N=N, G=G)
    yg_s, ssm_s = ssd_sample(zx, dtx, state_conv[0], state_ssm[0].reshape(Bs, G, di // G, N), cw, cb,
                             row(jnp.repeat(m_dt_bias[0], P)), row(jnp.repeat(m_A_log[0], P)), dx, ng,
                             B=Bs, L=Ls, row0=Tp, G=G, N=N)
    h = fused_matmul(yg_p, m_out_proj[0].astype(bf16), xb=yg_s, tm=256, tn=512, mode="resnorm", res=h,
                     g_post=row(mix_post_g[0]), name="out_proj")
    h = ffn(h, 0, 1)

    xbc = zx[:, di:]
    conv_prompt = xbc[:Tp].reshape(Bp, S, conv_dim)[:, S - (CONV_W - 1):][None]
    conv_sample = xbc[Tp:].reshape(Bs, Ls, conv_dim)[:, Ls - (CONV_W - 1):][None]
    ssm_prompt = ssm_p.reshape(1, Bp, H, P, N)
    ssm_sample = ssm_s.reshape(1, Bs, H, P, N)

    w_kv = jnp.concatenate([w_dkv, _swap_halves(w_dkv[:, C:], dr)], axis=1).astype(bf16)
    lat, kr, krb = fused_matmul(h, w_kv, tm=tm, mode="kv", g_pre=row(kv_norm_g), g_post=row(kv_latent_norm_g),
                                cos=cos64, sin=sin64, n_lat=C, name="shared_kv")

    h = ffn(h, 1, 0)
    cq = fused_matmul(h, q_w_dq[0].astype(bf16), tm=tm, mode="norm", g_pre=row(mix_pre_g[1]),
                      g_post=row(q_norm_g[0]), out_dtype=bf16, name="q_down")
    w_uq = q_w_uq[0]
    qrank = w_uq.shape[0]
    w_qn = w_uq[:, :, :dn].reshape(qrank, Hm * dn).astype(bf16)
    w_qr = w_uq[:, :, dn:].reshape(qrank, Hm * dr)
    w_qr2 = jnp.concatenate([w_qr, _swap_halves(w_qr, dr)], axis=1).astype(bf16)
    qn = fused_matmul(cq, w_qn, tm=tm, scale=scale, out_dtype=bf16, name="q_nope")
    qr = fused_matmul(cq, w_qr2, tm=tm, mode="rope", cos=cos128, sin=sin128, scale=scale, out_dtype=bf16,
                      name="q_rope")

    w_kvup = jnp.concatenate([w_uk.reshape(C, Hm * dn), w_uv.reshape(C, Hm * dv)], axis=1).astype(bf16)
    kv = fused_matmul(lat, w_kvup, tm=tm, tn=1024, n_rows=Tp, out_dtype=bf16, name="kv_up")
    qr_hm = qr.reshape(T, Hm, dr).transpose(1, 0, 2)
    o_p = attn_prompt(qn, qr_hm, kv, krb, B=Bp, S=S, H=Hm, dn=dn, dv=dv, tq=512)

    w_ukh = w_uk.transpose(1, 2, 0).astype(bf16)
    w_uvh = w_uv.transpose(1, 0, 2).astype(bf16)
    ql = head_matmul(qn[Tp:], w_ukh, name="q_absorb")
    o_lat = attn_sample(page_table, ql.reshape(Bs, Ls * Hm, C), qr[Tp:].reshape(Bs, Ls * Hm, dr),
                        lat[Tp:].reshape(Bs, Ls, C), kr[Tp:].reshape(Bs, Ls, dr),
                        cache_kv_latent, cache_k_rope, H=Hm, pages_per_step=8)
    o_s = head_matmul(o_lat.reshape(Ts, Hm * C), w_uvh, name="v_up")

    h = fused_matmul(o_p, attn_w_o[0].reshape(Hm * dv, D).astype(bf16), xb=o_s, tm=tm, tn=512, mode="resnorm",
                     res=h, g_post=row(mix_post_g[1]), name="attn_out")
    h = ffn(h, 1, 1)

    y_prompt = h[:Tp].reshape(Bp, S, D)
    y_sample = h[Tp:].reshape(Bs, Ls, D)
    return (y_prompt, y_sample, lat[:Tp].reshape(Bp, S, C), kr[:Tp].reshape(Bp, S, dr), ssm_prompt, conv_prompt,
            lat[Tp:].reshape(Bs, Ls, C), kr[Tp:].reshape(Bs, Ls, dr), ssm_sample, conv_sample)
```

```python
import functools

import jax
import jax.numpy as jnp
from jax import lax
from jax.experimental import pallas as pl
from jax.experimental.pallas import tpu as pltpu

f32 = jnp.float32
bf16 = jnp.bfloat16

RMS_EPS = 1e-6
ROPE_THETA = 10000.0
SSD_CHUNK = 128
SSM_GROUPS = 8
CONV_W = 4
LANES = 128
V7X_VMEM_BYTES = 64 * 1024 * 1024
VMEM_LIMIT = V7X_VMEM_BYTES - 8 * 1024 * 1024

_NT = (((1,), (1,)), ((), ()))
_TN = (((0,), (0,)), ((), ()))


def _rms(x, g):
    return x * lax.rsqrt(jnp.mean(x * x, axis=-1, keepdims=True) + RMS_EPS) * g


def _silu(x):
    return x * jax.nn.sigmoid(x)


def _softplus(x):
    return jnp.maximum(x, 0.0) + jnp.log1p(jnp.exp(-jnp.abs(x)))


def _params(*sem):
    return pltpu.CompilerParams(dimension_semantics=sem, vmem_limit_bytes=VMEM_LIMIT)


def _ffn_kernel(h_ref, gpre_ref, wg_ref, wu_ref, wd_ref, gpost_ref, o_ref, xn_ref, acc_ref, *, nj, dchunk):
    j = pl.program_id(1)

    @pl.when(j == 0)
    def _():
        xn_ref[...] = _rms(h_ref[...], gpre_ref[...]).astype(bf16)
        acc_ref[...] = jnp.zeros_like(acc_ref)

    xn = xn_ref[...]
    g = jnp.dot(xn, wg_ref[...], preferred_element_type=f32)
    u = jnp.dot(xn, wu_ref[...], preferred_element_type=f32)
    a = (_silu(g) * u).astype(bf16)
    for c in range(0, acc_ref.shape[1], dchunk):
        acc_ref[:, c:c + dchunk] += jnp.dot(a, wd_ref[:, c:c + dchunk], preferred_element_type=f32)

    @pl.when(j == nj - 1)
    def _():
        o_ref[...] = h_ref[...] + 0.5 * _rms(acc_ref[...], gpost_ref[...])


def ffn_half(h, g_pre, wg, wu, wd, g_post, *, tm, tf):
    T, D = h.shape
    F = wg.shape[1]
    nj = F // tf
    return pl.pallas_call(
        functools.partial(_ffn_kernel, nj=nj, dchunk=min(D, 512)),
        grid=(T // tm, nj),
        in_specs=[
            pl.BlockSpec((tm, D), lambda i, j: (i, 0)),
            pl.BlockSpec((1, D), lambda i, j: (0, 0)),
            pl.BlockSpec((D, tf), lambda i, j: (0, j)),
            pl.BlockSpec((D, tf), lambda i, j: (0, j)),
            pl.BlockSpec((tf, D), lambda i, j: (j, 0)),
            pl.BlockSpec((1, D), lambda i, j: (0, 0)),
        ],
        out_specs=pl.BlockSpec((tm, D), lambda i, j: (i, 0)),
        out_shape=jax.ShapeDtypeStruct((T, D), f32),
        scratch_shapes=[pltpu.VMEM((tm, D), bf16), pltpu.VMEM((tm, D), f32)],
        compiler_params=_params("parallel", "arbitrary"),
        name="ffn_half",
    )(h, g_pre, wg, wu, wd, g_post)


def _mm_kernel(*refs, prenorm, dual, na, mode, nj, tn, scale, use_xn, n_lat):
    refs = list(refs)
    xa_ref = refs.pop(0)
    xb_ref = refs.pop(0) if dual else None
    gpre_ref = refs.pop(0) if prenorm else None
    w_ref = refs.pop(0)
    i = pl.program_id(0)
    j = pl.program_id(1)

    def load_x(x_ref):
        x = x_ref[...]
        if prenorm:
            x = _rms(x.astype(f32), gpre_ref[...])
        return x.astype(bf16)

    if use_xn:
        xn_ref = refs[-1] if mode in ("plain", "kv", "rope") or nj == 1 else refs[-2]

        if dual:
            @pl.when((j == 0) & (i < na))
            def _():
                xn_ref[...] = load_x(xa_ref)

            @pl.when((j == 0) & (i >= na))
            def _():
                xn_ref[...] = load_x(xb_ref)
        else:
            @pl.when(j == 0)
            def _():
                xn_ref[...] = load_x(xa_ref)

        xn = xn_ref[...]
    else:
        xn = xa_ref[...]
    acc = jnp.dot(xn, w_ref[...], preferred_element_type=f32)

    if mode == "plain":
        o_ref = refs[0]
        o_ref[...] = (acc * scale if scale != 1.0 else acc).astype(o_ref.dtype)
    elif mode in ("norm", "resnorm"):
        if mode == "resnorm":
            res_ref, gpost_ref, o_ref = refs[0], refs[1], refs[2]
        else:
            res_ref, gpost_ref, o_ref = None, refs[0], refs[1]

        def finish(slabs):
            n = nj * tn
            ssq = sum(jnp.sum(s * s, axis=-1, keepdims=True) for s in slabs)
            inv = lax.rsqrt(ssq * (1.0 / n) + RMS_EPS)
            for k, s in enumerate(slabs):
                y = s * inv * gpost_ref[:, k * tn:(k + 1) * tn]
                if res_ref is not None:
                    y = res_ref[:, k * tn:(k + 1) * tn] + y
                o_ref[:, k * tn:(k + 1) * tn] = y.astype(o_ref.dtype)

        if nj == 1:
            finish([acc])
        else:
            slab_ref = refs[-1]
            slab_ref[j] = acc

            @pl.when(j == nj - 1)
            def _():
                finish([slab_ref[k] for k in range(nj)])
    elif mode == "kv":
        gpost_ref, cos_ref, sin_ref, lat_ref, kr_ref, krb_ref = refs[:6]
        lat_ref[...] = _rms(acc[:, :n_lat], gpost_ref[...])
        r = cos_ref.shape[1]
        kr = acc[:, n_lat:n_lat + r] * cos_ref[...] + acc[:, n_lat + r:n_lat + 2 * r] * sin_ref[...]
        kr_ref[...] = kr
        krb_ref[...] = kr.astype(bf16)
    elif mode == "rope":
        cos_ref, sin_ref, o_ref = refs[:3]
        half = acc.shape[1] // 2
        reps = half // cos_ref.shape[1]
        cos = jnp.concatenate([cos_ref[...]] * reps, axis=1)
        sin = jnp.concatenate([sin_ref[...]] * reps, axis=1)
        o_ref[...] = ((acc[:, :half] * cos + acc[:, half:] * sin) * scale).astype(o_ref.dtype)
    else:
        raise ValueError(mode)


def fused_matmul(x, w, *, tm, tn=None, mode="plain", g_pre=None, xb=None, res=None, g_post=None,
                 cos=None, sin=None, scale=1.0, out_dtype=f32, n_rows=None, row_block_offset=0,
                 n_lat=0, name="fused_matmul"):
    K, N = w.shape
    if tn is None:
        tn = N
    nj = N // tn
    dual = xb is not None
    na = x.shape[0] // tm
    M = (x.shape[0] + (xb.shape[0] if dual else 0)) if n_rows is None else n_rows
    ni = M // tm
    off = row_block_offset
    prenorm = g_pre is not None
    use_xn = prenorm or dual or x.dtype != bf16

    args, in_specs = [], []
    if dual:
        args += [x, xb]
        in_specs += [pl.BlockSpec((tm, K), lambda i, j: (jnp.minimum(i, na - 1), 0)),
                     pl.BlockSpec((tm, K), lambda i, j: (jnp.maximum(i - na, 0), 0))]
    else:
        args.append(x)
        in_specs.append(pl.BlockSpec((tm, K), lambda i, j: (i + off, 0)))
    if prenorm:
        args.append(g_pre)
        in_specs.append(pl.BlockSpec((1, K), lambda i, j: (0, 0)))
    args.append(w)
    in_specs.append(pl.BlockSpec((K, tn), lambda i, j: (0, j)))

    scratch = []
    if mode == "plain":
        out_shape = jax.ShapeDtypeStruct((M, N), out_dtype)
        out_specs = pl.BlockSpec((tm, tn), lambda i, j: (i, j))
    elif mode in ("norm", "resnorm"):
        if mode == "resnorm":
            args.append(res)
            in_specs.append(pl.BlockSpec((tm, N), lambda i, j: (i + off, 0)))
        args.append(g_post)
        in_specs.append(pl.BlockSpec((1, N), lambda i, j: (0, 0)))
        out_shape = jax.ShapeDtypeStruct((M, N), out_dtype)
        out_specs = pl.BlockSpec((tm, N), lambda i, j: (i, 0))
    elif mode == "kv":
        assert nj == 1
        r = cos.shape[1]
        args += [g_post, cos, sin]
        in_specs += [pl.BlockSpec((1, n_lat), lambda i, j: (0, 0)),
                     pl.BlockSpec((tm, r), lambda i, j: (i + off, 0)),
                     pl.BlockSpec((tm, r), lambda i, j: (i + off, 0))]
        out_shape = (jax.ShapeDtypeStruct((M, n_lat), f32), jax.ShapeDtypeStruct((M, r), f32),
                     jax.ShapeDtypeStruct((M, r), bf16))
        out_specs = (pl.BlockSpec((tm, n_lat), lambda i, j: (i, 0)), pl.BlockSpec((tm, r), lambda i, j: (i, 0)),
                     pl.BlockSpec((tm, r), lambda i, j: (i, 0)))
    elif mode == "rope":
        assert nj == 1
        r = cos.shape[1]
        args += [cos, sin]
        in_specs += [pl.BlockSpec((tm, r), lambda i, j: (i + off, 0)),
                     pl.BlockSpec((tm, r), lambda i, j: (i + off, 0))]
        out_shape = jax.ShapeDtypeStruct((M, N // 2), out_dtype)
        out_specs = pl.BlockSpec((tm, N // 2), lambda i, j: (i, 0))
    else:
        raise ValueError(mode)
    if mode in ("norm", "resnorm") and nj > 1:
        if use_xn:
            scratch.append(pltpu.VMEM((tm, K), bf16))
        scratch.append(pltpu.VMEM((nj, tm, tn), f32))
    elif use_xn:
        scratch.append(pltpu.VMEM((tm, K), bf16))

    kern = functools.partial(_mm_kernel, prenorm=prenorm, dual=dual, na=na, mode=mode, nj=nj, tn=tn,
                             scale=scale, use_xn=use_xn, n_lat=n_lat)
    return pl.pallas_call(
        kern, grid=(ni, nj), in_specs=in_specs, out_specs=out_specs, out_shape=out_shape,
        scratch_shapes=scratch, compiler_params=_params("parallel", "arbitrary"), name=name,
    )(*args)


def _bmm_kernel(x_ref, w_ref, o_ref):
    o_ref[...] = jnp.dot(x_ref[...], w_ref[0], preferred_element_type=f32).astype(o_ref.dtype)


def head_matmul(x, w, *, out_dtype=bf16, name="head_matmul"):
    H, K, N = w.shape
    M = x.shape[0]
    return pl.pallas_call(
        _bmm_kernel, grid=(H,),
        in_specs=[pl.BlockSpec((M, K), lambda h: (0, h)), pl.BlockSpec((1, K, N), lambda h: (h, 0, 0))],
        out_specs=pl.BlockSpec((M, N), lambda h: (0, h)),
        out_shape=jax.ShapeDtypeStruct((M, H * N), out_dtype),
        compiler_params=_params("parallel"), name=name,
    )(x, w)


def _conv_silu(prev, cur, w_ref, b_ref, n_prev):
    L = cur.shape[0]
    xp = jnp.concatenate([prev, cur], axis=0)
    base = n_prev - (CONV_W - 1)
    acc = b_ref[...] + xp[base:base + L] * w_ref[0:1, :]
    for k in range(1, CONV_W):
        acc = acc + xp[base + k:base + k + L] * w_ref[k:k + 1, :]
    return _silu(acc)


def _gated_group_norm(y, z, g):
    v = y * _silu(z)
    return v * lax.rsqrt(jnp.mean(v * v, axis=-1, keepdims=True) + RMS_EPS) * g


def _ssd_prompt_kernel(z_ref, x_ref, bc_ref, dt_ref, cwx_ref, cwbc_ref, cbx_ref, cbbc_ref, dtb_ref, alog_ref,
                       dx_ref, ng_ref, yg_ref, st_ref, xprev_ref, bcprev_ref, *, L, H, N, G):
    c = pl.program_id(1)
    n_prev = xprev_ref.shape[0]
    gw = G * N
    hpg = H // G
    cpg = x_ref.shape[1] // G

    @pl.when(c == 0)
    def _():
        st_ref[...] = jnp.zeros_like(st_ref)
        xprev_ref[...] = jnp.zeros_like(xprev_ref)
        bcprev_ref[...] = jnp.zeros_like(bcprev_ref)

    xcur = x_ref[...]
    bccur = bc_ref[...]
    xc = _conv_silu(xprev_ref[...], xcur, cwx_ref, cbx_ref, n_prev)
    bcc = _conv_silu(bcprev_ref[...], bccur, cwbc_ref, cbbc_ref, n_prev)
    xprev_ref[...] = xcur[L - n_prev:]
    bcprev_ref[...] = bccur[L - n_prev:]

    dt = _softplus(dt_ref[...] + dtb_ref[...])
    a = dt * (-jnp.exp(alog_ref[...]))
    row = lax.broadcasted_iota(jnp.int32, (L, L), 0)
    col = lax.broadcasted_iota(jnp.int32, (L, L), 1)
    causal = row >= col
    acs = jnp.dot(causal.astype(f32), a, preferred_element_type=f32, precision=lax.Precision.HIGHEST)
    lane = lax.broadcasted_iota(jnp.int32, (L, LANES), 1)
    sub = lax.broadcasted_iota(jnp.int32, (LANES, LANES), 0)
    rows = jnp.where(lane < LANES // 2, acs, pltpu.roll(dt, LANES // 2, 1)).T
    acs_last = acs[L - 1:L, :]
    eacs = jnp.exp(acs)
    seg = jnp.exp(acs_last - acs) * dt
    dec = jnp.exp(acs_last)

    lo = lane < LANES // 2
    for g in range(G):
        Bg = bcc[:, g * N:(g + 1) * N]
        Cg = bcc[:, gw + g * N:gw + (g + 1) * N]
        cb = lax.dot_general(Cg.astype(bf16), Bg.astype(bf16), _NT, preferred_element_type=f32)
        ys = []
        for pp in range(hpg // 2):
            h0 = g * hpg + 2 * pp
            q = h0 // 2
            ms, cs, bw = [], [], []
            for e in (h0, h0 + 1):
                diff = acs[:, e:e + 1] - rows[e:e + 1, :]
                decay = jnp.exp(jnp.where(causal, diff, -jnp.inf))
                ms.append(cb * decay * rows[LANES // 2 + e:LANES // 2 + e + 1, :])
                cs.append(Cg * eacs[:, e:e + 1])
                bw.append(Bg * seg[:, e:e + 1])
            xpair = xc[:, q * LANES:(q + 1) * LANES]
            xbd = jnp.concatenate([jnp.where(lo, xpair, 0.0), jnp.where(lo, 0.0, xpair)], axis=0).astype(bf16)
            hp = st_ref[0, q]
            hbd = jnp.concatenate([jnp.where(sub < LANES // 2, hp, 0.0), jnp.where(sub < LANES // 2, 0.0, hp)],
                                  axis=1).astype(bf16)
            y = jnp.dot(jnp.concatenate(ms, axis=1).astype(bf16), xbd, preferred_element_type=f32)
            y = y + lax.dot_general(jnp.concatenate(cs, axis=1).astype(bf16), hbd, _NT, preferred_element_type=f32)
            y = y + xpair * dx_ref[:, q * LANES:(q + 1) * LANES]
            st = lax.dot_general(xbd, jnp.concatenate(bw, axis=0).astype(bf16), _TN, preferred_element_type=f32)
            decp = jnp.where(sub < LANES // 2, dec[:, h0:h0 + 1], dec[:, h0 + 1:h0 + 2])
            st_ref[0, q] = decp * hp + st
            ys.append(y)
        sl = slice(g * cpg, (g + 1) * cpg)
        yg_ref[:, sl] = _gated_group_norm(jnp.concatenate(ys, axis=1), z_ref[:, sl], ng_ref[:, sl]).astype(yg_ref.dtype)


def ssd_prompt(zx, dtp, cw, cb, dtb, alog, dx, ng, *, B, S, H, P, N, G):
    L = SSD_CHUNK
    nc = S // L
    di = H * P
    gw = 2 * G * N
    assert 2 * P == LANES and N == LANES and di % gw == 0 and H <= LANES // 2
    cwx, cwbc = cw[:, :di], cw[:, di:]
    cbx, cbbc = cb[:, :di], cb[:, di:]
    kern = functools.partial(_ssd_prompt_kernel, L=L, H=H, N=N, G=G)
    const = lambda b, c: (0, 0)
    return pl.pallas_call(
        kern, grid=(B, nc),
        in_specs=[
            pl.BlockSpec((L, di), lambda b, c: (b * nc + c, 0)),
            pl.BlockSpec((L, di), lambda b, c: (b * nc + c, 1)),
            pl.BlockSpec((L, gw), lambda b, c: (b * nc + c, 2 * di // gw)),
            pl.BlockSpec((L, LANES), lambda b, c: (b * nc + c, 0)),
            pl.BlockSpec((CONV_W, di), const), pl.BlockSpec((CONV_W, gw), const),
            pl.BlockSpec((1, di), const), pl.BlockSpec((1, gw), const),
            pl.BlockSpec((1, LANES), const), pl.BlockSpec((1, LANES), const),
            pl.BlockSpec((1, di), const), pl.BlockSpec((1, di), const),
        ],
        out_specs=(pl.BlockSpec((L, di), lambda b, c: (b * nc + c, 0)),
                   pl.BlockSpec((1, H // 2, 2 * P, N), lambda b, c: (b, 0, 0, 0))),
        out_shape=(jax.ShapeDtypeStruct((B * S, di), bf16), jax.ShapeDtypeStruct((B, H // 2, 2 * P, N), f32)),
        scratch_shapes=[pltpu.VMEM((8, di), f32), pltpu.VMEM((8, gw), f32)],
        compiler_params=_params("parallel", "arbitrary"), name="ssd_prompt",
    )(zx, zx, zx, dtp, cwx, cwbc, cbx, cbbc, dtb, alog, dx, ng)


def _ssd_sample_kernel(z_ref, x_ref, bc_ref, dtx_ref, cpx_ref, cpbc_ref, h0_ref, cwx_ref, cwbc_ref, cbx_ref,
                       cbbc_ref, dtb_ref, alog_ref, dx_ref, ng_ref, yg_ref, st_ref, *, L, N, G):
    di = x_ref.shape[1]
    gw = G * N
    cpg = di // G
    xc = _conv_silu(cpx_ref[0], x_ref[...], cwx_ref, cbx_ref, CONV_W - 1)
    bcc = _conv_silu(cpbc_ref[0], bc_ref[...], cwbc_ref, cbbc_ref, CONV_W - 1)
    dt = _softplus(dtx_ref[...] + dtb_ref[...])
    a = dt * (-jnp.exp(alog_ref[...]))
    row = lax.broadcasted_iota(jnp.int32, (L, di), 0)
    acs = jnp.zeros((L, di), f32)
    for s in range(L):
        acs = acs + jnp.where(row >= s, jnp.broadcast_to(a[s:s + 1, :], (L, di)), 0.0)
    acs_last = acs[L - 1:L, :]
    xdt = xc * dt

    cbx = []
    yoff = []
    for g in range(G):
        Bg = bcc[:, g * N:(g + 1) * N].astype(bf16)
        Cg = bcc[:, gw + g * N:gw + (g + 1) * N].astype(bf16)
        cb = lax.dot_general(Cg, Bg, _NT, preferred_element_type=f32)
        cbx.append([jnp.broadcast_to(cb[:, s:s + 1], (L, cpg)) for s in range(L)])
        hg = h0_ref[0, g]
        yoff.append(lax.dot_general(Cg, hg.astype(bf16), _NT, preferred_element_type=f32))
        sl = slice(g * cpg, (g + 1) * cpg)
        xd = (xdt[:, sl] * jnp.exp(acs_last[:, sl] - acs[:, sl])).astype(bf16)
        st = lax.dot_general(xd, Bg, _TN, preferred_element_type=f32)
        dcol = jnp.exp(jnp.broadcast_to(acs_last[:, sl], (L, cpg)).T[:, 0:1])
        st_ref[0, g] = dcol * hg + st
    y = jnp.exp(acs) * jnp.concatenate(yoff, axis=1) + xc * dx_ref[...]
    for s in range(L):
        decay = jnp.exp(jnp.where(row >= s, acs - jnp.broadcast_to(acs[s:s + 1, :], (L, di)), -jnp.inf))
        cbs = jnp.concatenate([cbx[g][s] for g in range(G)], axis=1)
        y = y + cbs * decay * jnp.broadcast_to(xdt[s:s + 1, :], (L, di))
    for g in range(G):
        sl = slice(g * cpg, (g + 1) * cpg)
        yg_ref[:, sl] = _gated_group_norm(y[:, sl], z_ref[:, sl], ng_ref[:, sl]).astype(yg_ref.dtype)


def ssd_sample(zx, dtx, conv_prev, h0, cw, cb, dtbx, alogx, dx, ng, *, B, L, row0, G, N):
    di = dtx.shape[1]
    gw = 2 * G * N
    cpg = di // G
    cwx, cwbc = cw[:, :di], cw[:, di:]
    cbx, cbbc = cb[:, :di], cb[:, di:]
    rb = row0 // L
    kern = functools.partial(_ssd_sample_kernel, L=L, N=N, G=G)
    const = lambda b: (0, 0)
    return pl.pallas_call(
        kern, grid=(B,),
        in_specs=[
            pl.BlockSpec((L, di), lambda b: (rb + b, 0)),
            pl.BlockSpec((L, di), lambda b: (rb + b, 1)),
            pl.BlockSpec((L, gw), lambda b: (rb + b, 2 * di // gw)),
            pl.BlockSpec((L, di), lambda b: (b, 0)),
            pl.BlockSpec((1, CONV_W - 1, di), lambda b: (b, 0, 0)),
            pl.BlockSpec((1, CONV_W - 1, gw), lambda b: (b, 0, di // gw)),
            pl.BlockSpec((1, G, cpg, N), lambda b: (b, 0, 0, 0)),
            pl.BlockSpec((CONV_W, di), const), pl.BlockSpec((CONV_W, gw), const),
            pl.BlockSpec((1, di), const), pl.BlockSpec((1, gw), const),
            pl.BlockSpec((1, di), const), pl.BlockSpec((1, di), const),
            pl.BlockSpec((1, di), const), pl.BlockSpec((1, di), const),
        ],
        out_specs=(pl.BlockSpec((L, di), lambda b: (b, 0)),
                   pl.BlockSpec((1, G, cpg, N), lambda b: (b, 0, 0, 0))),
        out_shape=(jax.ShapeDtypeStruct((B * L, di), f32), jax.ShapeDtypeStruct(h0.shape, f32)),
        compiler_params=_params("parallel"), name="ssd_sample",
    )(zx, zx, zx, dtx, conv_prev, conv_prev, h0, cwx, cwbc, cbx, cbbc, dtbx, alogx, dx, ng)


def _attn_prompt_kernel(qn_ref, qr_ref, kn_ref, v_ref, kr_ref, o_ref, kf_ref, *, tq):
    qi = pl.program_id(2)
    dn, dr, dk, dv = kn_ref.shape[1], kr_ref.shape[1], kf_ref.shape[1], v_ref.shape[1]

    @pl.when(qi == 0)
    def _():
        kf_ref[:, :dn] = kn_ref[...]
        kf_ref[:, dn:dn + dr] = kr_ref[...]
        kf_ref[:, dn + dr:] = jnp.zeros((kf_ref.shape[0], dk - dn - dr), bf16)

    q = jnp.concatenate([qn_ref[...], qr_ref[0], jnp.zeros((tq, dk - dn - dr), bf16)], axis=1)

    def block(carry, k0, diagonal=False):
        ks = pl.ds(pl.multiple_of(k0, tq), tq)
        s = lax.dot_general(q, kf_ref[ks, :], _NT, preferred_element_type=f32)
        if diagonal:
            r = lax.broadcasted_iota(jnp.int32, (tq, tq), 0)
            c = lax.broadcasted_iota(jnp.int32, (tq, tq), 1)
            s = jnp.where(r >= c, s, -jnp.inf)
        m, l, acc = carry
        m_new = jnp.maximum(m, jnp.max(s, axis=-1, keepdims=True))
        alpha = jnp.exp(m - m_new)
        p = jnp.exp(s - m_new)
        l = l * alpha + jnp.sum(p, axis=-1, keepdims=True)
        acc = acc * alpha + jnp.dot(p.astype(bf16), v_ref[ks, :], preferred_element_type=f32)
        return m_new, l, acc

    def pair(i, carry):
        return block(block(carry, 2 * i * tq), (2 * i + 1) * tq)

    carry = (jnp.full((tq, 1), -jnp.inf, f32), jnp.zeros((tq, 1), f32), jnp.zeros((tq, dv), f32))
    carry = lax.fori_loop(0, qi // 2, pair, carry)
    carry = lax.cond(qi % 2 == 1, lambda c: block(c, (qi - 1) * tq), lambda c: c, carry)
    _, l, acc = block(carry, qi * tq, diagonal=True)
    o_ref[...] = (acc / l).astype(o_ref.dtype)


def attn_prompt(qn, qr_hm, kv, krb, *, B, S, H, dn, dv, tq):
    nq = S // tq
    dr = krb.shape[1]
    assert dn == dv
    return pl.pallas_call(
        functools.partial(_attn_prompt_kernel, tq=tq), grid=(B, H, nq),
        in_specs=[
            pl.BlockSpec((tq, dn), lambda b, h, i: (b * nq + i, h)),
            pl.BlockSpec((1, tq, dr), lambda b, h, i: (h, b * nq + i, 0)),
            pl.BlockSpec((S, dn), lambda b, h, i: (b, h)),
            pl.BlockSpec((S, dv), lambda b, h, i: (b, H + h)),
            pl.BlockSpec((S, dr), lambda b, h, i: (b, 0)),
        ],
        out_specs=pl.BlockSpec((tq, dv), lambda b, h, i: (b * nq + i, h)),
        out_shape=jax.ShapeDtypeStruct((B * S, H * dv), bf16),
        scratch_shapes=[pltpu.VMEM((S, 2 * LANES), bf16)],
        compiler_params=_params("parallel", "parallel", "arbitrary"), name="attn_prompt",
    )(qn, qr_hm, kv, kv, krb)


def _attn_sample_kernel(pt_ref, *refs, npg, P, H):
    del pt_ref
    ql_ref, qr_ref, cn_ref, krn_ref = refs[:4]
    lat_refs = refs[4:4 + P]
    krt_refs = refs[4 + P:4 + 2 * P]
    o_ref, m_ref, l_ref, acc_ref, cb_ref, kt_ref = refs[4 + 2 * P:]
    j = pl.program_id(1)
    ql = ql_ref[0]
    qr = qr_ref[0]
    R = ql.shape[0]

    @pl.when(j == 0)
    def _():
        cn = cn_ref[0].astype(bf16)
        S = cn.shape[0]
        s = (lax.dot_general(ql, cn, _NT, preferred_element_type=f32)
             + lax.dot_general(qr, krn_ref[0].astype(bf16), _NT, preferred_element_type=f32))
        qpos = lax.broadcasted_iota(jnp.int32, (R, S), 0) // H
        kpos = lax.broadcasted_iota(jnp.int32, (R, S), 1)
        s = jnp.where(kpos <= qpos, s, -jnp.inf)
        m = jnp.max(s, axis=-1, keepdims=True)
        p = jnp.exp(s - m)
        m_ref[...] = m
        l_ref[...] = jnp.sum(p, axis=-1, keepdims=True)
        acc_ref[...] = jnp.dot(p.astype(bf16), cn, preferred_element_type=f32)

    pg = lat_refs[0].shape[1]
    for k in range(P):
        cb_ref[k * pg:(k + 1) * pg, :] = lat_refs[k][0].astype(bf16)
        kt_ref[:, k * pg:(k + 1) * pg] = krt_refs[k][0].astype(bf16)
    cb = cb_ref[...]
    s = (lax.dot_general(ql, cb, _NT, preferred_element_type=f32)
         + jnp.dot(qr, kt_ref[...], preferred_element_type=f32))
    m = m_ref[...]
    m_new = jnp.maximum(m, jnp.max(s, axis=-1, keepdims=True))
    alpha = jnp.exp(m - m_new)
    p = jnp.exp(s - m_new)
    m_ref[...] = m_new
    l_ref[...] = l_ref[...] * alpha + jnp.sum(p, axis=-1, keepdims=True)
    acc_ref[...] = acc_ref[...] * alpha + jnp.dot(p.astype(bf16), cb, preferred_element_type=f32)

    @pl.when(j == npg - 1)
    def _():
        o_ref[0] = (acc_ref[...] / l_ref[...]).astype(o_ref.dtype)


def attn_sample(page_table, ql, qr, c_new, kr_new, cache_lat, cache_krt, *, H, pages_per_step):
    B, R, C = ql.shape
    dr = qr.shape[2]
    S = c_new.shape[1]
    n_pages = page_table.shape[1]
    P = pages_per_step
    npg = n_pages // P
    pg = cache_lat.shape[1]
    pt = page_table.reshape(-1)

    def page_map(k):
        return lambda b, j, pt_ref: (pt_ref[b * n_pages + j * P + k], 0, 0)

    seq = lambda b, j, pt_ref: (b, 0, 0)
    in_specs = [pl.BlockSpec((1, R, C), seq), pl.BlockSpec((1, R, dr), seq),
                pl.BlockSpec((1, S, C), seq), pl.BlockSpec((1, S, dr), seq)]
    in_specs += [pl.BlockSpec((1, pg, C), page_map(k)) for k in range(P)]
    in_specs += [pl.BlockSpec((1, dr, pg), page_map(k)) for k in range(P)]
    return pl.pallas_call(
        functools.partial(_attn_sample_kernel, npg=npg, P=P, H=H),
        grid_spec=pltpu.PrefetchScalarGridSpec(
            num_scalar_prefetch=1, grid=(B, npg), in_specs=in_specs,
            out_specs=pl.BlockSpec((1, R, C), seq),
            scratch_shapes=[pltpu.VMEM((R, 1), f32), pltpu.VMEM((R, 1), f32), pltpu.VMEM((R, C), f32),
                            pltpu.VMEM((P * pg, C), bf16), pltpu.VMEM((dr, P * pg), bf16)]),
        out_shape=jax.ShapeDtypeStruct((B, R, C), bf16),
        compiler_params=_params("parallel", "arbitrary"), name="attn_sample",
    )(pt, ql, qr, c_new, kr_new, *([cache_lat] * P), *([cache_krt] * P))


def _rope_tables(pos, half):
    inv = ROPE_THETA ** (-jnp.arange(half, dtype=f32) / half)
    ang = pos[:, None] * inv[None, :]
    cos, sin = jnp.cos(ang), jnp.sin(ang)
    return jnp.concatenate([cos, cos], axis=1), jnp.concatenate([-sin, sin], axis=1)


def _swap_halves(w, r):
    k, n = w.shape
    return w.reshape(k, n // r, 2, r // 2)[:, :, ::-1, :].reshape(k, n)


def kernel(x_prompt, x_sample, cache_kv_latent, cache_k_rope, state_ssm, state_conv, page_table, ffn_pre_g, ffn_post_g, ffn_w_gate, ffn_w_up, ffn_w_down, mix_pre_g, mix_post_g, m_in_proj, m_conv_w, m_conv_b, m_dt_bias, m_A_log, m_D, m_norm_g, m_out_proj, kv_norm_g, w_dkv, kv_latent_norm_g, w_uk, w_uv, q_w_dq, q_norm_g, q_w_uq, attn_w_o):
    Bp, S, D = x_prompt.shape
    Bs, Ls, _ = x_sample.shape
    Tp, Ts = Bp * S, Bs * Ls
    T = Tp + Ts
    depth = ffn_pre_g.shape[0]
    n_a = state_ssm.shape[0]
    assert depth == 2 and n_a == 1, "layer pattern: one SSD layer then one MLA layer"
    H, P, N = state_ssm.shape[2:]
    G = SSM_GROUPS
    di = H * P
    conv_dim = state_conv.shape[3]
    C = cache_kv_latent.shape[2]
    dr = cache_k_rope.shape[2]
    Hm, dn = w_uk.shape[1:]
    dv = w_uv.shape[2]
    past_len = page_table.shape[1] * cache_kv_latent.shape[1]
    scale = float(dn + dr) ** -0.5
    tm = 512
    row = lambda v: v.reshape(1, -1)

    def ffn(h, layer, j):
        return ffn_half(h, row(ffn_pre_g[layer, j]), ffn_w_gate[layer, j].astype(bf16),
                        ffn_w_up[layer, j].astype(bf16), ffn_w_down[layer, j].astype(bf16),
                        row(ffn_post_g[layer, j]), tm=tm, tf=512)

    pos = jnp.concatenate([jnp.tile(jnp.arange(S, dtype=f32), Bp),
                           jnp.tile(past_len + jnp.arange(Ls, dtype=f32), Bs)])
    cos64, sin64 = _rope_tables(pos, dr // 2)
    cos128, sin128 = jnp.tile(cos64, (1, 2)), jnp.tile(sin64, (1, 2))

    h = jnp.concatenate([x_prompt.reshape(Tp, D), x_sample.reshape(Ts, D)], axis=0)

    h = ffn(h, 0, 0)
    w_in = m_in_proj[0]
    nzx = di + conv_dim
    w_zx = w_in[:, :nzx].astype(bf16)
    w_dt = w_in[:, nzx:]
    w_dtp = jnp.pad(w_dt, ((0, 0), (0, LANES - H))).astype(bf16)
    w_dtx = jnp.repeat(w_dt, P, axis=1).astype(bf16)
    g_mix0 = row(mix_pre_g[0])
    zx = fused_matmul(h, w_zx, tm=tm, tn=1024, g_pre=g_mix0, name="in_proj_zx")
    dtp = fused_matmul(h, w_dtp, tm=tm, g_pre=g_mix0, n_rows=Tp, name="in_proj_dt")
    dtx = fused_matmul(h, w_dtx, tm=tm, tn=1024, g_pre=g_mix0, n_rows=Ts, row_block_offset=Tp // tm,
                       name="in_proj_dtx")
    cw, cb = m_conv_w[0], row(m_conv_b[0])
    dx = row(jnp.repeat(m_D[0], P))
    ng = row(m_norm_g[0])
    pad_h = lambda v: row(jnp.pad(v, (0, LANES - H)))
    yg_p, ssm_p = ssd_prompt(zx, dtp, cw, cb, pad_h(m_dt_bias[0]), pad_h(m_A_log[0]), dx, ng,
                             B=Bp, S=S, H=H, P=P, N=N, G=G)
    yg_s, ssm_s = ssd_sample(zx, dtx, state_conv[0], state_ssm[0].reshape(Bs, G, di // G, N), cw, cb,
                             row(jnp.repeat(m_dt_bias[0], P)), row(jnp.repeat(m_A_log[0], P)), dx, ng,
                             B=Bs, L=Ls, row0=Tp, G=G, N=N)
    h = fused_matmul(yg_p, m_out_proj[0].astype(bf16), xb=yg_s.astype(bf16), tm=tm, tn=512, mode="resnorm", res=h,
                     g_post=row(mix_post_g[0]), name="out_proj")
    h = ffn(h, 0, 1)

    tail = CONV_W - 1
    conv_prompt = zx[:Tp].reshape(Bp, S, -1)[:, S - tail:, di:][None]
    conv_sample = zx[Tp:].reshape(Bs, Ls, -1)[:, Ls - tail:, di:][None]
    ssm_prompt = ssm_p.reshape(1, Bp, H, P, N)
    ssm_sample = ssm_s.reshape(1, Bs, H, P, N)

    w_kv = jnp.concatenate([w_dkv, _swap_halves(w_dkv[:, C:], dr)], axis=1).astype(bf16)
    lat, kr, krb = fused_matmul(h, w_kv, tm=tm, mode="kv", g_pre=row(kv_norm_g), g_post=row(kv_latent_norm_g),
                                cos=cos64, sin=sin64, n_lat=C, name="shared_kv")

    h = ffn(h, 1, 0)
    cq = fused_matmul(h, q_w_dq[0].astype(bf16), tm=tm, mode="norm", g_pre=row(mix_pre_g[1]),
                      g_post=row(q_norm_g[0]), out_dtype=bf16, name="q_down")
    w_uq = q_w_uq[0]
    qrank = w_uq.shape[0]
    w_qn = w_uq[:, :, :dn].reshape(qrank, Hm * dn).astype(bf16)
    w_qr = w_uq[:, :, dn:].reshape(qrank, Hm * dr)
    w_qr2 = jnp.concatenate([w_qr, _swap_halves(w_qr, dr)], axis=1).astype(bf16)
    qn = fused_matmul(cq, w_qn, tm=tm, scale=scale, out_dtype=bf16, name="q_nope")
    qr = fused_matmul(cq, w_qr2, tm=tm, mode="rope", cos=cos128, sin=sin128, scale=scale, out_dtype=bf16,
                      name="q_rope")

    w_kvup = jnp.concatenate([w_uk.reshape(C, Hm * dn), w_uv.reshape(C, Hm * dv)], axis=1).astype(bf16)
    kv = fused_matmul(lat, w_kvup, tm=tm, tn=1024, n_rows=Tp, out_dtype=bf16, name="kv_up")
    qr_hm = qr.reshape(T, Hm, dr).transpose(1, 0, 2)
    o_p = attn_prompt(qn, qr_hm, kv, krb, B=Bp, S=S, H=Hm, dn=dn, dv=dv, tq=512)

    w_ukh = w_uk.transpose(1, 2, 0).astype(bf16)
    w_uvh = w_uv.transpose(1, 0, 2).astype(bf16)
    ql = head_matmul(qn[Tp:], w_ukh, name="q_absorb")
    o_lat = attn_sample(page_table, ql.reshape(Bs, Ls * Hm, C), qr[Tp:].reshape(Bs, Ls * Hm, dr),
                        lat[Tp:].reshape(Bs, Ls, C), kr[Tp:].reshape(Bs, Ls, dr),
                        cache_kv_latent, cache_k_rope.transpose(0, 2, 1), H=Hm,
                        pages_per_step=min(32, page_table.shape[1]))
    o_s = head_matmul(o_lat.reshape(Ts, Hm * C), w_uvh, name="v_up")

    h = fused_matmul(o_p, attn_w_o[0].reshape(Hm * dv, D).astype(bf16), xb=o_s, tm=tm, tn=512, mode="resnorm",
                     res=h, g_post=row(mix_post_g[1]), name="attn_out")
    h = ffn(h, 1, 1)

    y_prompt = h[:Tp].reshape(Bp, S, D)
    y_sample = h[Tp:].reshape(Bs, Ls, D)
    return (y_prompt, y_sample, lat[:Tp].reshape(Bp, S, C), kr[:Tp].reshape(Bp, S, dr), ssm_prompt, conv_prompt,
            lat[Tp:].reshape(Bs, Ls, C), kr[Tp:].reshape(Bs, Ls, dr), ssm_sample, conv_sample)
```

```python
import functools

import jax
import jax.numpy as jnp
from jax import lax
from jax.experimental import pallas as pl
from jax.experimental.pallas import tpu as pltpu

f32 = jnp.float32
bf16 = jnp.bfloat16

RMS_EPS = 1e-6
ROPE_THETA = 10000.0
SSD_CHUNK = 128
SSM_GROUPS = 8
CONV_W = 4
LANES = 128
V7X_VMEM_BYTES = 64 * 1024 * 1024
VMEM_LIMIT = V7X_VMEM_BYTES - 8 * 1024 * 1024

_NT = (((1,), (1,)), ((), ()))
_TN = (((0,), (0,)), ((), ()))


def _rms(x, g):
    return x * lax.rsqrt(jnp.mean(x * x, axis=-1, keepdims=True) + RMS_EPS) * g


def _silu(x):
    return x * jax.nn.sigmoid(x)


def _softplus(x):
    return jnp.maximum(x, 0.0) + jnp.log1p(jnp.exp(-jnp.abs(x)))


def _params(*sem):
    return pltpu.CompilerParams(dimension_semantics=sem, vmem_limit_bytes=VMEM_LIMIT)


def _ffn_kernel(h_ref, gpre_ref, wg_ref, wu_ref, wd_ref, gpost_ref, o_ref, xn_ref, acc_ref, *, nj, dchunk):
    j = pl.program_id(1)

    @pl.when(j == 0)
    def _():
        xn_ref[...] = _rms(h_ref[...], gpre_ref[...]).astype(bf16)
        acc_ref[...] = jnp.zeros_like(acc_ref)

    xn = xn_ref[...]
    g = jnp.dot(xn, wg_ref[...], preferred_element_type=f32)
    u = jnp.dot(xn, wu_ref[...], preferred_element_type=f32)
    a = (_silu(g) * u).astype(bf16)
    for c in range(0, acc_ref.shape[1], dchunk):
        acc_ref[:, c:c + dchunk] += jnp.dot(a, wd_ref[:, c:c + dchunk], preferred_element_type=f32)

    @pl.when(j == nj - 1)
    def _():
        o_ref[...] = h_ref[...] + 0.5 * _rms(acc_ref[...], gpost_ref[...])


def ffn_half(h, g_pre, wg, wu, wd, g_post, *, sel, tm, tf):
    T, D = h.shape
    F = wg.shape[3]
    nj = F // tf
    return pl.pallas_call(
        functools.partial(_ffn_kernel, nj=nj, dchunk=min(D, 512)),
        grid=(T // tm, nj),
        in_specs=[
            pl.BlockSpec((tm, D), lambda i, j: (i, 0)),
            pl.BlockSpec((1, D), lambda i, j: (0, 0)),
            pl.BlockSpec((None, None, D, tf), lambda i, j: (*sel, 0, j)),
            pl.BlockSpec((None, None, D, tf), lambda i, j: (*sel, 0, j)),
            pl.BlockSpec((None, None, tf, D), lambda i, j: (*sel, j, 0)),
            pl.BlockSpec((1, D), lambda i, j: (0, 0)),
        ],
        out_specs=pl.BlockSpec((tm, D), lambda i, j: (i, 0)),
        out_shape=jax.ShapeDtypeStruct((T, D), f32),
        scratch_shapes=[pltpu.VMEM((tm, D), bf16), pltpu.VMEM((tm, D), f32)],
        compiler_params=_params("parallel", "arbitrary"),
        name="ffn_half",
    )(h, g_pre, wg, wu, wd, g_post)


def _mm_kernel(*refs, prenorm, dual, na, mode, nj, tn, scale, use_xn, n_lat):
    refs = list(refs)
    xa_ref = refs.pop(0)
    xb_ref = refs.pop(0) if dual else None
    gpre_ref = refs.pop(0) if prenorm else None
    w_ref = refs.pop(0)
    i = pl.program_id(0)
    j = pl.program_id(1)

    def load_x(x_ref):
        x = x_ref[...]
        if prenorm:
            x = _rms(x.astype(f32), gpre_ref[...])
        return x.astype(bf16)

    if use_xn:
        xn_ref = refs[-1] if mode in ("plain", "kv", "rope") or nj == 1 else refs[-2]

        if dual:
            @pl.when((j == 0) & (i < na))
            def _():
                xn_ref[...] = load_x(xa_ref)

            @pl.when((j == 0) & (i >= na))
            def _():
                xn_ref[...] = load_x(xb_ref)
        else:
            @pl.when(j == 0)
            def _():
                xn_ref[...] = load_x(xa_ref)

        xn = xn_ref[...]
    else:
        xn = xa_ref[...]
    acc = jnp.dot(xn, w_ref[...], preferred_element_type=f32)

    if mode == "plain":
        o_ref = refs[0]
        o_ref[...] = (acc * scale if scale != 1.0 else acc).astype(o_ref.dtype)
    elif mode in ("norm", "resnorm"):
        if mode == "resnorm":
            res_ref, gpost_ref, o_ref = refs[0], refs[1], refs[2]
        else:
            res_ref, gpost_ref, o_ref = None, refs[0], refs[1]

        def finish(slabs):
            n = nj * tn
            ssq = sum(jnp.sum(s * s, axis=-1, keepdims=True) for s in slabs)
            inv = lax.rsqrt(ssq * (1.0 / n) + RMS_EPS)
            for k, s in enumerate(slabs):
                y = s * inv * gpost_ref[:, k * tn:(k + 1) * tn]
                if res_ref is not None:
                    y = res_ref[:, k * tn:(k + 1) * tn] + y
                o_ref[:, k * tn:(k + 1) * tn] = y.astype(o_ref.dtype)

        if nj == 1:
            finish([acc])
        else:
            slab_ref = refs[-1]
            slab_ref[j] = acc

            @pl.when(j == nj - 1)
            def _():
                finish([slab_ref[k] for k in range(nj)])
    elif mode == "kv":
        gpost_ref, cos_ref, sin_ref, lat_ref, kr_ref, krb_ref = refs[:6]
        lat_ref[...] = _rms(acc[:, :n_lat], gpost_ref[...])
        r = cos_ref.shape[1]
        kr = acc[:, n_lat:n_lat + r] * cos_ref[...] + acc[:, n_lat + r:n_lat + 2 * r] * sin_ref[...]
        kr_ref[...] = kr
        krb_ref[...] = kr.astype(bf16)
    elif mode == "rope":
        cos_ref, sin_ref, o_ref = refs[:3]
        half = acc.shape[1] // 2
        reps = half // cos_ref.shape[1]
        cos = jnp.concatenate([cos_ref[...]] * reps, axis=1)
        sin = jnp.concatenate([sin_ref[...]] * reps, axis=1)
        o_ref[...] = ((acc[:, :half] * cos + acc[:, half:] * sin) * scale).astype(o_ref.dtype)
    else:
        raise ValueError(mode)


def fused_matmul(x, w, *, tm, tn=None, mode="plain", g_pre=None, xb=None, res=None, g_post=None,
                 cos=None, sin=None, scale=1.0, out_dtype=f32, n_rows=None, row_block_offset=0,
                 n_lat=0, name="fused_matmul"):
    K, N = w.shape
    if tn is None:
        tn = N
    nj = N // tn
    dual = xb is not None
    na = x.shape[0] // tm
    M = (x.shape[0] + (xb.shape[0] if dual else 0)) if n_rows is None else n_rows
    ni = M // tm
    off = row_block_offset
    prenorm = g_pre is not None
    use_xn = prenorm or dual or x.dtype != bf16

    args, in_specs = [], []
    if dual:
        args += [x, xb]
        in_specs += [pl.BlockSpec((tm, K), lambda i, j: (jnp.minimum(i, na - 1), 0)),
                     pl.BlockSpec((tm, K), lambda i, j: (jnp.maximum(i - na, 0), 0))]
    else:
        args.append(x)
        in_specs.append(pl.BlockSpec((tm, K), lambda i, j: (i + off, 0)))
    if prenorm:
        args.append(g_pre)
        in_specs.append(pl.BlockSpec((1, K), lambda i, j: (0, 0)))
    args.append(w)
    in_specs.append(pl.BlockSpec((K, tn), lambda i, j: (0, j)))

    scratch = []
    if mode == "plain":
        out_shape = jax.ShapeDtypeStruct((M, N), out_dtype)
        out_specs = pl.BlockSpec((tm, tn), lambda i, j: (i, j))
    elif mode in ("norm", "resnorm"):
        if mode == "resnorm":
            args.append(res)
            in_specs.append(pl.BlockSpec((tm, N), lambda i, j: (i + off, 0)))
        args.append(g_post)
        in_specs.append(pl.BlockSpec((1, N), lambda i, j: (0, 0)))
        out_shape = jax.ShapeDtypeStruct((M, N), out_dtype)
        out_specs = pl.BlockSpec((tm, N), lambda i, j: (i, 0))
    elif mode == "kv":
        assert nj == 1
        r = cos.shape[1]
        args += [g_post, cos, sin]
        in_specs += [pl.BlockSpec((1, n_lat), lambda i, j: (0, 0)),
                     pl.BlockSpec((tm, r), lambda i, j: (i + off, 0)),
                     pl.BlockSpec((tm, r), lambda i, j: (i + off, 0))]
        out_shape = (jax.ShapeDtypeStruct((M, n_lat), f32), jax.ShapeDtypeStruct((M, r), f32),
                     jax.ShapeDtypeStruct((M, r), bf16))
        out_specs = (pl.BlockSpec((tm, n_lat), lambda i, j: (i, 0)), pl.BlockSpec((tm, r), lambda i, j: (i, 0)),
                     pl.BlockSpec((tm, r), lambda i, j: (i, 0)))
    elif mode == "rope":
        assert nj == 1
        r = cos.shape[1]
        args += [cos, sin]
        in_specs += [pl.BlockSpec((tm, r), lambda i, j: (i + off, 0)),
                     pl.BlockSpec((tm, r), lambda i, j: (i + off, 0))]
        out_shape = jax.ShapeDtypeStruct((M, N // 2), out_dtype)
        out_specs = pl.BlockSpec((tm, N // 2), lambda i, j: (i, 0))
    else:
        raise ValueError(mode)
    if mode in ("norm", "resnorm") and nj > 1:
        if use_xn:
            scratch.append(pltpu.VMEM((tm, K), bf16))
        scratch.append(pltpu.VMEM((nj, tm, tn), f32))
    elif use_xn:
        scratch.append(pltpu.VMEM((tm, K), bf16))

    kern = functools.partial(_mm_kernel, prenorm=prenorm, dual=dual, na=na, mode=mode, nj=nj, tn=tn,
                             scale=scale, use_xn=use_xn, n_lat=n_lat)
    return pl.pallas_call(
        kern, grid=(ni, nj), in_specs=in_specs, out_specs=out_specs, out_shape=out_shape,
        scratch_shapes=scratch, compiler_params=_params("parallel", "arbitrary"), name=name,
    )(*args)


def _bmm_kernel(x_ref, w_ref, o_ref):
    o_ref[...] = jnp.dot(x_ref[...], w_ref[0], preferred_element_type=f32).astype(o_ref.dtype)


def head_matmul(x, w, *, out_dtype=bf16, name="head_matmul"):
    H, K, N = w.shape
    M = x.shape[0]
    return pl.pallas_call(
        _bmm_kernel, grid=(H,),
        in_specs=[pl.BlockSpec((M, K), lambda h: (0, h)), pl.BlockSpec((1, K, N), lambda h: (h, 0, 0))],
        out_specs=pl.BlockSpec((M, N), lambda h: (0, h)),
        out_shape=jax.ShapeDtypeStruct((M, H * N), out_dtype),
        compiler_params=_params("parallel"), name=name,
    )(x, w)


def _conv_silu(prev, cur, w_ref, b_ref, n_prev):
    L = cur.shape[0]
    xp = jnp.concatenate([prev, cur], axis=0)
    base = n_prev - (CONV_W - 1)
    acc = b_ref[...] + xp[base:base + L] * w_ref[0:1, :]
    for k in range(1, CONV_W):
        acc = acc + xp[base + k:base + k + L] * w_ref[k:k + 1, :]
    return _silu(acc)


def _gated_group_norm(y, z, g):
    v = y * _silu(z)
    return v * lax.rsqrt(jnp.mean(v * v, axis=-1, keepdims=True) + RMS_EPS) * g


def _expand_heads(v, e_ref):
    hi = v.astype(bf16)
    lo = (v - hi.astype(f32)).astype(bf16)
    e = e_ref[...]
    return jnp.dot(hi, e, preferred_element_type=f32) + jnp.dot(lo, e, preferred_element_type=f32)


def _ssd_prompt_kernel(z_ref, x_ref, bc_ref, dt_ref, cwx_ref, cwbc_ref, cbx_ref, cbbc_ref, dtb_ref, alog_ref,
                       dx_ref, ng_ref, e_ref, yg_ref, st_ref, xprev_ref, bcprev_ref, *, L, H, N, G):
    c = pl.program_id(1)
    n_prev = xprev_ref.shape[0]
    gw = G * N
    hpg = H // G
    cpg = x_ref.shape[1] // G
    P = cpg // hpg

    @pl.when(c == 0)
    def _():
        st_ref[...] = jnp.zeros_like(st_ref)
        xprev_ref[...] = jnp.zeros_like(xprev_ref)
        bcprev_ref[...] = jnp.zeros_like(bcprev_ref)

    xcur = x_ref[...]
    bccur = bc_ref[...]
    xc = _conv_silu(xprev_ref[...], xcur, cwx_ref, cbx_ref, n_prev)
    bcc = _conv_silu(bcprev_ref[...], bccur, cwbc_ref, cbbc_ref, n_prev)
    xprev_ref[...] = xcur[L - n_prev:]
    bcprev_ref[...] = bccur[L - n_prev:]

    dt = _softplus(dt_ref[...] + dtb_ref[...])
    a = dt * (-jnp.exp(alog_ref[...]))
    row = lax.broadcasted_iota(jnp.int32, (L, L), 0)
    col = lax.broadcasted_iota(jnp.int32, (L, L), 1)
    causal = row >= col
    acs = jnp.dot(causal.astype(f32), a, preferred_element_type=f32, precision=lax.Precision.HIGHEST)
    lane = lax.broadcasted_iota(jnp.int32, (L, LANES), 1)
    rows = jnp.where(lane < LANES // 2, acs, pltpu.roll(dt, LANES // 2, 1)).T
    acs_last = acs[L - 1:L, :]
    dec = jnp.exp(acs_last)
    eacs_x = _expand_heads(jnp.exp(acs), e_ref)
    seg_x = _expand_heads(jnp.exp(acs_last - acs) * dt, e_ref)
    xs = (xc * seg_x).astype(bf16)

    lo = lane < LANES // 2
    for g in range(G):
        sl = slice(g * cpg, (g + 1) * cpg)
        Bg = bcc[:, g * N:(g + 1) * N].astype(bf16)
        Cg = bcc[:, gw + g * N:gw + (g + 1) * N].astype(bf16)
        cb = lax.dot_general(Cg, Bg, _NT, preferred_element_type=f32)
        hg = st_ref[0, g]
        y = lax.dot_general(Cg, hg.astype(bf16), _NT, preferred_element_type=f32) * eacs_x[:, sl]
        y = y + xc[:, sl] * dx_ref[:, sl]
        ys = []
        for pp in range(hpg // 2):
            h0 = g * hpg + 2 * pp
            ms = []
            for e in (h0, h0 + 1):
                diff = acs[:, e:e + 1] - rows[e:e + 1, :]
                decay = jnp.exp(jnp.where(causal, diff, -jnp.inf))
                ms.append(cb * decay * rows[LANES // 2 + e:LANES // 2 + e + 1, :])
            xpair = xc[:, (h0 // 2) * LANES:(h0 // 2 + 1) * LANES]
            xbd = jnp.concatenate([jnp.where(lo, xpair, 0.0), jnp.where(lo, 0.0, xpair)], axis=0).astype(bf16)
            ys.append(jnp.dot(jnp.concatenate(ms, axis=1).astype(bf16), xbd, preferred_element_type=f32))
        y = y + jnp.concatenate(ys, axis=1)
        st = lax.dot_general(xs[:, sl], Bg, _TN, preferred_element_type=f32)
        decg = jnp.concatenate([jnp.broadcast_to(dec[:, h:h + 1], (P, N)) for h in range(g * hpg, (g + 1) * hpg)],
                               axis=0)
        st_ref[0, g] = decg * hg + st
        yg_ref[:, sl] = _gated_group_norm(y, z_ref[:, sl], ng_ref[:, sl]).astype(yg_ref.dtype)


def ssd_prompt(zx, dtp, cw, cb, dtb, alog, dx, ng, *, B, S, H, P, N, G):
    L = SSD_CHUNK
    nc = S // L
    di = H * P
    gw = 2 * G * N
    assert 2 * P == LANES and N == LANES and di % gw == 0 and H <= LANES // 2
    cwx, cwbc = cw[:, :di], cw[:, di:]
    cbx, cbbc = cb[:, :di], cb[:, di:]
    expand = (jnp.arange(LANES)[:, None] == jnp.arange(di)[None, :] // P).astype(bf16)
    kern = functools.partial(_ssd_prompt_kernel, L=L, H=H, N=N, G=G)
    const = lambda b, c: (0, 0)
    return pl.pallas_call(
        kern, grid=(B, nc),
        in_specs=[
            pl.BlockSpec((L, di), lambda b, c: (b * nc + c, 0)),
            pl.BlockSpec((L, di), lambda b, c: (b * nc + c, 1)),
            pl.BlockSpec((L, gw), lambda b, c: (b * nc + c, 2 * di // gw)),
            pl.BlockSpec((L, LANES), lambda b, c: (b * nc + c, 0)),
            pl.BlockSpec((CONV_W, di), const), pl.BlockSpec((CONV_W, gw), const),
            pl.BlockSpec((1, di), const), pl.BlockSpec((1, gw), const),
            pl.BlockSpec((1, LANES), const), pl.BlockSpec((1, LANES), const),
            pl.BlockSpec((1, di), const), pl.BlockSpec((1, di), const),
            pl.BlockSpec((LANES, di), const),
        ],
        out_specs=(pl.BlockSpec((L, di), lambda b, c: (b * nc + c, 0)),
                   pl.BlockSpec((1, G, di // G, N), lambda b, c: (b, 0, 0, 0))),
        out_shape=(jax.ShapeDtypeStruct((B * S, di), bf16), jax.ShapeDtypeStruct((B, G, di // G, N), f32)),
        scratch_shapes=[pltpu.VMEM((8, di), f32), pltpu.VMEM((8, gw), f32)],
        compiler_params=_params("parallel", "arbitrary"), name="ssd_prompt",
    )(zx, zx, zx, dtp, cwx, cwbc, cbx, cbbc, dtb, alog, dx, ng, expand)


def _ssd_sample_kernel(z_ref, x_ref, bc_ref, dtx_ref, cpx_ref, cpbc_ref, h0_ref, cwx_ref, cwbc_ref, cbx_ref,
                       cbbc_ref, dtb_ref, alog_ref, dx_ref, ng_ref, yg_ref, st_ref, *, L, N, G):
    di = x_ref.shape[1]
    gw = G * N
    cpg = di // G
    xc = _conv_silu(cpx_ref[0], x_ref[...], cwx_ref, cbx_ref, CONV_W - 1)
    bcc = _conv_silu(cpbc_ref[0], bc_ref[...], cwbc_ref, cbbc_ref, CONV_W - 1)
    dt = _softplus(dtx_ref[...] + dtb_ref[...])
    a = dt * (-jnp.exp(alog_ref[...]))
    row = lax.broadcasted_iota(jnp.int32, (L, di), 0)
    acs = jnp.zeros((L, di), f32)
    for s in range(L):
        acs = acs + jnp.where(row >= s, jnp.broadcast_to(a[s:s + 1, :], (L, di)), 0.0)
    acs_last = acs[L - 1:L, :]
    xdt = xc * dt

    cbx = []
    yoff = []
    for g in range(G):
        Bg = bcc[:, g * N:(g + 1) * N].astype(bf16)
        Cg = bcc[:, gw + g * N:gw + (g + 1) * N].astype(bf16)
        cb = lax.dot_general(Cg, Bg, _NT, preferred_element_type=f32)
        cbx.append([jnp.broadcast_to(cb[:, s:s + 1], (L, cpg)) for s in range(L)])
        hg = h0_ref[0, g]
        yoff.append(lax.dot_general(Cg, hg.astype(bf16), _NT, preferred_element_type=f32))
        sl = slice(g * cpg, (g + 1) * cpg)
        xd = (xdt[:, sl] * jnp.exp(acs_last[:, sl] - acs[:, sl])).astype(bf16)
        st = lax.dot_general(xd, Bg, _TN, preferred_element_type=f32)
        dcol = jnp.exp(jnp.broadcast_to(acs_last[:, sl], (L, cpg)).T[:, 0:1])
        st_ref[0, g] = dcol * hg + st
    y = jnp.exp(acs) * jnp.concatenate(yoff, axis=1) + xc * dx_ref[...]
    for s in range(L):
        decay = jnp.exp(jnp.where(row >= s, acs - jnp.broadcast_to(acs[s:s + 1, :], (L, di)), -jnp.inf))
        cbs = jnp.concatenate([cbx[g][s] for g in range(G)], axis=1)
        y = y + cbs * decay * jnp.broadcast_to(xdt[s:s + 1, :], (L, di))
    for g in range(G):
        sl = slice(g * cpg, (g + 1) * cpg)
        yg_ref[:, sl] = _gated_group_norm(y[:, sl], z_ref[:, sl], ng_ref[:, sl]).astype(yg_ref.dtype)


def ssd_sample(zx, dtx, conv_prev, h0, cw, cb, dtbx, alogx, dx, ng, *, B, L, row0, G, N):
    di = dtx.shape[1]
    gw = 2 * G * N
    cpg = di // G
    cwx, cwbc = cw[:, :di], cw[:, di:]
    cbx, cbbc = cb[:, :di], cb[:, di:]
    rb = row0 // L
    kern = functools.partial(_ssd_sample_kernel, L=L, N=N, G=G)
    const = lambda b: (0, 0)
    return pl.pallas_call(
        kern, grid=(B,),
        in_specs=[
            pl.BlockSpec((L, di), lambda b: (rb + b, 0)),
            pl.BlockSpec((L, di), lambda b: (rb + b, 1)),
            pl.BlockSpec((L, gw), lambda b: (rb + b, 2 * di // gw)),
            pl.BlockSpec((L, di), lambda b: (b, 0)),
            pl.BlockSpec((1, CONV_W - 1, di), lambda b: (b, 0, 0)),
            pl.BlockSpec((1, CONV_W - 1, gw), lambda b: (b, 0, di // gw)),
            pl.BlockSpec((1, G, cpg, N), lambda b: (b, 0, 0, 0)),
            pl.BlockSpec((CONV_W, di), const), pl.BlockSpec((CONV_W, gw), const),
            pl.BlockSpec((1, di), const), pl.BlockSpec((1, gw), const),
            pl.BlockSpec((1, di), const), pl.BlockSpec((1, di), const),
            pl.BlockSpec((1, di), const), pl.BlockSpec((1, di), const),
        ],
        out_specs=(pl.BlockSpec((L, di), lambda b: (b, 0)),
                   pl.BlockSpec((1, G, cpg, N), lambda b: (b, 0, 0, 0))),
        out_shape=(jax.ShapeDtypeStruct((B * L, di), f32), jax.ShapeDtypeStruct(h0.shape, f32)),
        compiler_params=_params("parallel"), name="ssd_sample",
    )(zx, zx, zx, dtx, conv_prev, conv_prev, h0, cwx, cwbc, cbx, cbbc, dtbx, alogx, dx, ng)


def _attn_prompt_kernel(qn_ref, qr_ref, kn_ref, v_ref, kr_ref, o_ref, kf_ref, *, tq):
    qi = pl.program_id(2)
    dn, dr, dk, dv = kn_ref.shape[1], kr_ref.shape[1], kf_ref.shape[1], v_ref.shape[1]

    @pl.when(qi == 0)
    def _():
        kf_ref[:, :dn] = kn_ref[...]
        kf_ref[:, dn:dn + dr] = kr_ref[...]
        kf_ref[:, dn + dr:] = jnp.zeros((kf_ref.shape[0], dk - dn - dr), bf16)

    q = jnp.concatenate([qn_ref[...], qr_ref[0], jnp.zeros((tq, dk - dn - dr), bf16)], axis=1)

    def block(carry, k0, diagonal=False):
        ks = pl.ds(pl.multiple_of(k0, tq), tq)
        s = lax.dot_general(q, kf_ref[ks, :], _NT, preferred_element_type=f32)
        if diagonal:
            r = lax.broadcasted_iota(jnp.int32, (tq, tq), 0)
            c = lax.broadcasted_iota(jnp.int32, (tq, tq), 1)
            s = jnp.where(r >= c, s, -jnp.inf)
        m, l, acc = carry
        m_new = jnp.maximum(m, jnp.max(s, axis=-1, keepdims=True))
        alpha = jnp.exp(m - m_new)
        p = jnp.exp(s - m_new)
        l = l * alpha + jnp.sum(p, axis=-1, keepdims=True)
        acc = acc * alpha + jnp.dot(p.astype(bf16), v_ref[ks, :], preferred_element_type=f32)
        return m_new, l, acc

    def pair(i, carry):
        return block(block(carry, 2 * i * tq), (2 * i + 1) * tq)

    carry = (jnp.full((tq, 1), -jnp.inf, f32), jnp.zeros((tq, 1), f32), jnp.zeros((tq, dv), f32))
    carry = lax.fori_loop(0, qi // 2, pair, carry)
    carry = lax.cond(qi % 2 == 1, lambda c: block(c, (qi - 1) * tq), lambda c: c, carry)
    _, l, acc = block(carry, qi * tq, diagonal=True)
    o_ref[...] = (acc / l).astype(o_ref.dtype)


def attn_prompt(qn, qr_hm, kv, krb, *, B, S, H, dn, dv, tq):
    nq = S // tq
    dr = krb.shape[1]
    assert dn == dv
    return pl.pallas_call(
        functools.partial(_attn_prompt_kernel, tq=tq), grid=(B, H, nq),
        in_specs=[
            pl.BlockSpec((tq, dn), lambda b, h, i: (b * nq + i, h)),
            pl.BlockSpec((1, tq, dr), lambda b, h, i: (h, b * nq + i, 0)),
            pl.BlockSpec((S, dn), lambda b, h, i: (b, h)),
            pl.BlockSpec((S, dv), lambda b, h, i: (b, H + h)),
            pl.BlockSpec((S, dr), lambda b, h, i: (b, 0)),
        ],
        out_specs=pl.BlockSpec((tq, dv), lambda b, h, i: (b * nq + i, h)),
        out_shape=jax.ShapeDtypeStruct((B * S, H * dv), bf16),
        scratch_shapes=[pltpu.VMEM((S, 2 * LANES), bf16)],
        compiler_params=_params("parallel", "parallel", "arbitrary"), name="attn_prompt",
    )(qn, qr_hm, kv, kv, krb)


def _attn_sample_kernel(pt_ref, *refs, npg, P, H):
    del pt_ref
    ql_ref, qr_ref, cn_ref, krn_ref = refs[:4]
    lat_refs = refs[4:4 + P]
    krt_refs = refs[4 + P:4 + 2 * P]
    o_ref, m_ref, l_ref, acc_ref, cb_ref, kt_ref = refs[4 + 2 * P:]
    j = pl.program_id(1)
    ql = ql_ref[0]
    qr = qr_ref[0]
    R = ql.shape[0]

    @pl.when(j == 0)
    def _():
        cn = cn_ref[0].astype(bf16)
        S = cn.shape[0]
        s = (lax.dot_general(ql, cn, _NT, preferred_element_type=f32)
             + lax.dot_general(qr, krn_ref[0].astype(bf16), _NT, preferred_element_type=f32))
        qpos = lax.broadcasted_iota(jnp.int32, (R, S), 0) // H
        kpos = lax.broadcasted_iota(jnp.int32, (R, S), 1)
        s = jnp.where(kpos <= qpos, s, -jnp.inf)
        m = jnp.max(s, axis=-1, keepdims=True)
        p = jnp.exp(s - m)
        m_ref[...] = m
        l_ref[...] = jnp.sum(p, axis=-1, keepdims=True)
        acc_ref[...] = jnp.dot(p.astype(bf16), cn, preferred_element_type=f32)

    pg = lat_refs[0].shape[1]
    for k in range(P):
        cb_ref[k * pg:(k + 1) * pg, :] = lat_refs[k][0].astype(bf16)
        kt_ref[:, k * pg:(k + 1) * pg] = krt_refs[k][0].astype(bf16)
    cb = cb_ref[...]
    s = (lax.dot_general(ql, cb, _NT, preferred_element_type=f32)
         + jnp.dot(qr, kt_ref[...], preferred_element_type=f32))
    m = m_ref[...]
    m_new = jnp.maximum(m, jnp.max(s, axis=-1, keepdims=True))
    alpha = jnp.exp(m - m_new)
    p = jnp.exp(s - m_new)
    m_ref[...] = m_new
    l_ref[...] = l_ref[...] * alpha + jnp.sum(p, axis=-1, keepdims=True)
    acc_ref[...] = acc_ref[...] * alpha + jnp.dot(p.astype(bf16), cb, preferred_element_type=f32)

    @pl.when(j == npg - 1)
    def _():
        o_ref[0] = (acc_ref[...] / l_ref[...]).astype(o_ref.dtype)


def attn_sample(page_table, ql, qr, c_new, kr_new, cache_lat, cache_krt, *, H, pages_per_step):
    B, R, C = ql.shape
    dr = qr.shape[2]
    S = c_new.shape[1]
    n_pages = page_table.shape[1]
    P = pages_per_step
    npg = n_pages // P
    pg = cache_lat.shape[1]
    pt = page_table.reshape(-1)

    def page_map(k):
        return lambda b, j, pt_ref: (pt_ref[b * n_pages + j * P + k], 0, 0)

    seq = lambda b, j, pt_ref: (b, 0, 0)
    in_specs = [pl.BlockSpec((1, R, C), seq), pl.BlockSpec((1, R, dr), seq),
                pl.BlockSpec((1, S, C), seq), pl.BlockSpec((1, S, dr), seq)]
    in_specs += [pl.BlockSpec((1, pg, C), page_map(k)) for k in range(P)]
    in_specs += [pl.BlockSpec((1, dr, pg), page_map(k)) for k in range(P)]
    return pl.pallas_call(
        functools.partial(_attn_sample_kernel, npg=npg, P=P, H=H),
        grid_spec=pltpu.PrefetchScalarGridSpec(
            num_scalar_prefetch=1, grid=(B, npg), in_specs=in_specs,
            out_specs=pl.BlockSpec((1, R, C), seq),
            scratch_shapes=[pltpu.VMEM((R, 1), f32), pltpu.VMEM((R, 1), f32), pltpu.VMEM((R, C), f32),
                            pltpu.VMEM((P * pg, C), bf16), pltpu.VMEM((dr, P * pg), bf16)]),
        out_shape=jax.ShapeDtypeStruct((B, R, C), bf16),
        compiler_params=_params("parallel", "arbitrary"), name="attn_sample",
    )(pt, ql, qr, c_new, kr_new, *([cache_lat] * P), *([cache_krt] * P))


def _rope_tables(pos, half):
    inv = ROPE_THETA ** (-jnp.arange(half, dtype=f32) / half)
    ang = pos[:, None] * inv[None, :]
    cos, sin = jnp.cos(ang), jnp.sin(ang)
    return jnp.concatenate([cos, cos], axis=1), jnp.concatenate([-sin, sin], axis=1)


def _swap_halves(w, r):
    k, n = w.shape
    return w.reshape(k, n // r, 2, r // 2)[:, :, ::-1, :].reshape(k, n)


def kernel(x_prompt, x_sample, cache_kv_latent, cache_k_rope, state_ssm, state_conv, page_table, ffn_pre_g, ffn_post_g, ffn_w_gate, ffn_w_up, ffn_w_down, mix_pre_g, mix_post_g, m_in_proj, m_conv_w, m_conv_b, m_dt_bias, m_A_log, m_D, m_norm_g, m_out_proj, kv_norm_g, w_dkv, kv_latent_norm_g, w_uk, w_uv, q_w_dq, q_norm_g, q_w_uq, attn_w_o):
    Bp, S, D = x_prompt.shape
    Bs, Ls, _ = x_sample.shape
    Tp, Ts = Bp * S, Bs * Ls
    T = Tp + Ts
    depth = ffn_pre_g.shape[0]
    n_a = state_ssm.shape[0]
    assert depth == 2 and n_a == 1, "layer pattern: one SSD layer then one MLA layer"
    H, P, N = state_ssm.shape[2:]
    G = SSM_GROUPS
    di = H * P
    conv_dim = state_conv.shape[3]
    C = cache_kv_latent.shape[2]
    dr = cache_k_rope.shape[2]
    Hm, dn = w_uk.shape[1:]
    dv = w_uv.shape[2]
    past_len = page_table.shape[1] * cache_kv_latent.shape[1]
    scale = float(dn + dr) ** -0.5
    tm = 512
    row = lambda v: v.reshape(1, -1)

    wg_all, wu_all, wd_all = ffn_w_gate.astype(bf16), ffn_w_up.astype(bf16), ffn_w_down.astype(bf16)

    def ffn(h, layer, j):
        return ffn_half(h, row(ffn_pre_g[layer, j]), wg_all, wu_all, wd_all, row(ffn_post_g[layer, j]),
                        sel=(layer, j), tm=tm, tf=512)

    pos = jnp.concatenate([jnp.tile(jnp.arange(S, dtype=f32), Bp),
                           jnp.tile(past_len + jnp.arange(Ls, dtype=f32), Bs)])
    cos64, sin64 = _rope_tables(pos, dr // 2)
    cos128, sin128 = jnp.tile(cos64, (1, 2)), jnp.tile(sin64, (1, 2))

    h = jnp.concatenate([x_prompt.reshape(Tp, D), x_sample.reshape(Ts, D)], axis=0)

    h = ffn(h, 0, 0)
    w_in = m_in_proj[0]
    nzx = di + conv_dim
    w_zx = w_in[:, :nzx].astype(bf16)
    w_dt = w_in[:, nzx:]
    w_dtp = jnp.pad(w_dt, ((0, 0), (0, LANES - H))).astype(bf16)
    w_dtx = jnp.repeat(w_dt, P, axis=1).astype(bf16)
    g_mix0 = row(mix_pre_g[0])
    zx = fused_matmul(h, w_zx, tm=2 * tm, tn=1024, g_pre=g_mix0, name="in_proj_zx")
    dtp = fused_matmul(h, w_dtp, tm=tm, g_pre=g_mix0, n_rows=Tp, name="in_proj_dt")
    dtx = fused_matmul(h, w_dtx, tm=tm, tn=1024, g_pre=g_mix0, n_rows=Ts, row_block_offset=Tp // tm,
                       name="in_proj_dtx")
    cw, cb = m_conv_w[0], row(m_conv_b[0])
    dx = row(jnp.repeat(m_D[0], P))
    ng = row(m_norm_g[0])
    pad_h = lambda v: row(jnp.pad(v, (0, LANES - H)))
    yg_p, ssm_p = ssd_prompt(zx, dtp, cw, cb, pad_h(m_dt_bias[0]), pad_h(m_A_log[0]), dx, ng,
                             B=Bp, S=S, H=H, P=P, N=N, G=G)
    yg_s, ssm_s = ssd_sample(zx, dtx, state_conv[0], state_ssm[0].reshape(Bs, G, di // G, N), cw, cb,
                             row(jnp.repeat(m_dt_bias[0], P)), row(jnp.repeat(m_A_log[0], P)), dx, ng,
                             B=Bs, L=Ls, row0=Tp, G=G, N=N)
    h = fused_matmul(yg_p, m_out_proj[0].astype(bf16), xb=yg_s.astype(bf16), tm=tm, tn=512, mode="resnorm", res=h,
                     g_post=row(mix_post_g[0]), name="out_proj")
    h = ffn(h, 0, 1)

    tail = CONV_W - 1
    conv_prompt = jnp.stack([lax.slice(zx, ((b + 1) * S - tail, di), ((b + 1) * S, nzx)) for b in range(Bp)])[None]
    conv_sample = zx[Tp:].reshape(Bs, Ls, -1)[:, Ls - tail:, di:][None]
    ssm_prompt = ssm_p.reshape(1, Bp, H, P, N)
    ssm_sample = ssm_s.reshape(1, Bs, H, P, N)

    w_kv = jnp.concatenate([w_dkv, _swap_halves(w_dkv[:, C:], dr)], axis=1).astype(bf16)
    lat, kr, krb = fused_matmul(h, w_kv, tm=tm, mode="kv", g_pre=row(kv_norm_g), g_post=row(kv_latent_norm_g),
                                cos=cos64, sin=sin64, n_lat=C, name="shared_kv")

    h = ffn(h, 1, 0)
    cq = fused_matmul(h, q_w_dq[0].astype(bf16), tm=tm, mode="norm", g_pre=row(mix_pre_g[1]),
                      g_post=row(q_norm_g[0]), out_dtype=bf16, name="q_down")
    w_uq = q_w_uq[0]
    qrank = w_uq.shape[0]
    w_qn = w_uq[:, :, :dn].reshape(qrank, Hm * dn).astype(bf16)
    w_qr = w_uq[:, :, dn:].reshape(qrank, Hm * dr)
    w_qr2 = jnp.concatenate([w_qr, _swap_halves(w_qr, dr)], axis=1).astype(bf16)
    qn = fused_matmul(cq, w_qn, tm=tm, scale=scale, out_dtype=bf16, name="q_nope")
    qr = fused_matmul(cq, w_qr2, tm=tm, mode="rope", cos=cos128, sin=sin128, scale=scale, out_dtype=bf16,
                      name="q_rope")

    w_kvup = jnp.concatenate([w_uk.reshape(C, Hm * dn), w_uv.reshape(C, Hm * dv)], axis=1).astype(bf16)
    kv = fused_matmul(lat, w_kvup, tm=tm, tn=1024, n_rows=Tp, out_dtype=bf16, name="kv_up")
    qr_hm = qr.reshape(T, Hm, dr).transpose(1, 0, 2)
    o_p = attn_prompt(qn, qr_hm, kv, krb, B=Bp, S=S, H=Hm, dn=dn, dv=dv, tq=512)

    w_ukh = w_uk.transpose(1, 2, 0).astype(bf16)
    w_uvh = w_uv.transpose(1, 0, 2).astype(bf16)
    ql = head_matmul(qn[Tp:], w_ukh, name="q_absorb")
    o_lat = attn_sample(page_table, ql.reshape(Bs, Ls * Hm, C), qr[Tp:].reshape(Bs, Ls * Hm, dr),
                        lat[Tp:].reshape(Bs, Ls, C), kr[Tp:].reshape(Bs, Ls, dr),
                        cache_kv_latent, cache_k_rope.transpose(0, 2, 1), H=Hm,
                        pages_per_step=min(32, page_table.shape[1]))
    o_s = head_matmul(o_lat.reshape(Ts, Hm * C), w_uvh, name="v_up")

    h = fused_matmul(o_p, attn_w_o[0].reshape(Hm * dv, D).astype(bf16), xb=o_s, tm=tm, tn=512, mode="resnorm",
                     res=h, g_post=row(mix_post_g[1]), name="attn_out")
    h = ffn(h, 1, 1)

    y_prompt = h[:Tp].reshape(Bp, S, D)
    y_sample = h[Tp:].reshape(Bs, Ls, D)
    return (y_prompt, y_sample, lat[:Tp].reshape(Bp, S, C), kr[:Tp].reshape(Bp, S, dr), ssm_prompt, conv_prompt,
            lat[Tp:].reshape(Bs, Ls, C), kr[Tp:].reshape(Bs, Ls, dr), ssm_sample, conv_sample)
```

```python
import functools

import jax
import jax.numpy as jnp
from jax import lax
from jax.experimental import pallas as pl
from jax.experimental.pallas import tpu as pltpu

f32 = jnp.float32
bf16 = jnp.bfloat16

RMS_EPS = 1e-6
ROPE_THETA = 10000.0
SSD_CHUNK = 128
SSM_GROUPS = 8
CONV_W = 4
SAMPLE_SUB_PAGES = 8
LANES = 128
V7X_VMEM_BYTES = 64 * 1024 * 1024
VMEM_LIMIT = V7X_VMEM_BYTES - 8 * 1024 * 1024

_NT = (((1,), (1,)), ((), ()))
_TN = (((0,), (0,)), ((), ()))


def _rms(x, g):
    return x * lax.rsqrt(jnp.mean(x * x, axis=-1, keepdims=True) + RMS_EPS) * g


def _silu(x):
    return x * jax.nn.sigmoid(x)


def _softplus(x):
    return jnp.maximum(x, 0.0) + jnp.log1p(jnp.exp(-jnp.abs(x)))


def _params(*sem):
    return pltpu.CompilerParams(dimension_semantics=sem, vmem_limit_bytes=VMEM_LIMIT)


def _ffn_kernel(h_ref, gpre_ref, wg_ref, wu_ref, wd_ref, gpost_ref, o_ref, xn_ref, acc_ref, *, nj, dchunk):
    j = pl.program_id(1)

    @pl.when(j == 0)
    def _():
        xn_ref[...] = _rms(h_ref[...], gpre_ref[...]).astype(bf16)
        acc_ref[...] = jnp.zeros_like(acc_ref)

    xn = xn_ref[...]
    g = jnp.dot(xn, wg_ref[...], preferred_element_type=f32)
    u = jnp.dot(xn, wu_ref[...], preferred_element_type=f32)
    a = (_silu(g) * u).astype(bf16)
    for c in range(0, acc_ref.shape[1], dchunk):
        acc_ref[:, c:c + dchunk] += jnp.dot(a, wd_ref[:, c:c + dchunk], preferred_element_type=f32)

    @pl.when(j == nj - 1)
    def _():
        o_ref[...] = h_ref[...] + 0.5 * _rms(acc_ref[...], gpost_ref[...])


def ffn_half(h, g_pre, wg, wu, wd, g_post, *, sel, tm, tf):
    T, D = h.shape
    F = wg.shape[3]
    nj = F // tf
    return pl.pallas_call(
        functools.partial(_ffn_kernel, nj=nj, dchunk=min(D, 512)),
        grid=(T // tm, nj),
        in_specs=[
            pl.BlockSpec((tm, D), lambda i, j: (i, 0)),
            pl.BlockSpec((1, D), lambda i, j: (0, 0)),
            pl.BlockSpec((None, None, D, tf), lambda i, j: (*sel, 0, j)),
            pl.BlockSpec((None, None, D, tf), lambda i, j: (*sel, 0, j)),
            pl.BlockSpec((None, None, tf, D), lambda i, j: (*sel, j, 0)),
            pl.BlockSpec((1, D), lambda i, j: (0, 0)),
        ],
        out_specs=pl.BlockSpec((tm, D), lambda i, j: (i, 0)),
        out_shape=jax.ShapeDtypeStruct((T, D), f32),
        scratch_shapes=[pltpu.VMEM((tm, D), bf16), pltpu.VMEM((tm, D), f32)],
        compiler_params=_params("parallel", "arbitrary"),
        name="ffn_half",
    )(h, g_pre, wg, wu, wd, g_post)


def _mm_kernel(*refs, prenorm, dual, na, mode, nj, tn, scale, use_xn, n_lat):
    refs = list(refs)
    xa_ref = refs.pop(0)
    xb_ref = refs.pop(0) if dual else None
    gpre_ref = refs.pop(0) if prenorm else None
    w_ref = refs.pop(0)
    i = pl.program_id(0)
    j = pl.program_id(1)

    def load_x(x_ref):
        x = x_ref[...]
        if prenorm:
            x = _rms(x.astype(f32), gpre_ref[...])
        return x.astype(bf16)

    if use_xn:
        xn_ref = refs[-1] if mode in ("plain", "kv", "rope") or nj == 1 else refs[-2]

        if dual:
            @pl.when((j == 0) & (i < na))
            def _():
                xn_ref[...] = load_x(xa_ref)

            @pl.when((j == 0) & (i >= na))
            def _():
                xn_ref[...] = load_x(xb_ref)
        else:
            @pl.when(j == 0)
            def _():
                xn_ref[...] = load_x(xa_ref)

        xn = xn_ref[...]
    else:
        xn = xa_ref[...]
    acc = jnp.dot(xn, w_ref[...], preferred_element_type=f32)

    if mode == "plain":
        o_ref = refs[0]
        o_ref[...] = (acc * scale if scale != 1.0 else acc).astype(o_ref.dtype)
    elif mode in ("norm", "resnorm"):
        if mode == "resnorm":
            res_ref, gpost_ref, o_ref = refs[0], refs[1], refs[2]
        else:
            res_ref, gpost_ref, o_ref = None, refs[0], refs[1]

        def finish(slabs):
            n = nj * tn
            ssq = sum(jnp.sum(s * s, axis=-1, keepdims=True) for s in slabs)
            inv = lax.rsqrt(ssq * (1.0 / n) + RMS_EPS)
            for k, s in enumerate(slabs):
                y = s * inv * gpost_ref[:, k * tn:(k + 1) * tn]
                if res_ref is not None:
                    y = res_ref[:, k * tn:(k + 1) * tn] + y
                o_ref[:, k * tn:(k + 1) * tn] = y.astype(o_ref.dtype)

        if nj == 1:
            finish([acc])
        else:
            slab_ref = refs[-1]
            slab_ref[j] = acc

            @pl.when(j == nj - 1)
            def _():
                finish([slab_ref[k] for k in range(nj)])
    elif mode == "kv":
        gpost_ref, cos_ref, sin_ref, lat_ref, kr_ref, krb_ref = refs[:6]
        lat_ref[...] = _rms(acc[:, :n_lat], gpost_ref[...])
        r = cos_ref.shape[1]
        kr = acc[:, n_lat:n_lat + r] * cos_ref[...] + acc[:, n_lat + r:n_lat + 2 * r] * sin_ref[...]
        kr_ref[...] = kr
        krb_ref[...] = kr.astype(bf16)
    elif mode == "rope":
        cos_ref, sin_ref, o_ref = refs[:3]
        half = acc.shape[1] // 2
        reps = half // cos_ref.shape[1]
        cos = jnp.concatenate([cos_ref[...]] * reps, axis=1)
        sin = jnp.concatenate([sin_ref[...]] * reps, axis=1)
        o_ref[...] = ((acc[:, :half] * cos + acc[:, half:] * sin) * scale).astype(o_ref.dtype)
    else:
        raise ValueError(mode)


def fused_matmul(x, w, *, tm, tn=None, mode="plain", g_pre=None, xb=None, res=None, g_post=None,
                 cos=None, sin=None, scale=1.0, out_dtype=f32, n_rows=None, row_block_offset=0,
                 n_lat=0, name="fused_matmul"):
    K, N = w.shape
    if tn is None:
        tn = N
    nj = N // tn
    dual = xb is not None
    na = x.shape[0] // tm
    M = (x.shape[0] + (xb.shape[0] if dual else 0)) if n_rows is None else n_rows
    ni = M // tm
    off = row_block_offset
    prenorm = g_pre is not None
    use_xn = prenorm or dual or x.dtype != bf16

    args, in_specs = [], []
    if dual:
        args += [x, xb]
        in_specs += [pl.BlockSpec((tm, K), lambda i, j: (jnp.minimum(i, na - 1), 0)),
                     pl.BlockSpec((tm, K), lambda i, j: (jnp.maximum(i - na, 0), 0))]
    else:
        args.append(x)
        in_specs.append(pl.BlockSpec((tm, K), lambda i, j: (i + off, 0)))
    if prenorm:
        args.append(g_pre)
        in_specs.append(pl.BlockSpec((1, K), lambda i, j: (0, 0)))
    args.append(w)
    in_specs.append(pl.BlockSpec((K, tn), lambda i, j: (0, j)))

    scratch = []
    if mode == "plain":
        out_shape = jax.ShapeDtypeStruct((M, N), out_dtype)
        out_specs = pl.BlockSpec((tm, tn), lambda i, j: (i, j))
    elif mode in ("norm", "resnorm"):
        if mode == "resnorm":
            args.append(res)
            in_specs.append(pl.BlockSpec((tm, N), lambda i, j: (i + off, 0)))
        args.append(g_post)
        in_specs.append(pl.BlockSpec((1, N), lambda i, j: (0, 0)))
        out_shape = jax.ShapeDtypeStruct((M, N), out_dtype)
        out_specs = pl.BlockSpec((tm, N), lambda i, j: (i, 0))
    elif mode == "kv":
        assert nj == 1
        r = cos.shape[1]
        args += [g_post, cos, sin]
        in_specs += [pl.BlockSpec((1, n_lat), lambda i, j: (0, 0)),
                     pl.BlockSpec((tm, r), lambda i, j: (i + off, 0)),
                     pl.BlockSpec((tm, r), lambda i, j: (i + off, 0))]
        out_shape = (jax.ShapeDtypeStruct((M, n_lat), f32), jax.ShapeDtypeStruct((M, r), f32),
                     jax.ShapeDtypeStruct((M, r), bf16))
        out_specs = (pl.BlockSpec((tm, n_lat), lambda i, j: (i, 0)), pl.BlockSpec((tm, r), lambda i, j: (i, 0)),
                     pl.BlockSpec((tm, r), lambda i, j: (i, 0)))
    elif mode == "rope":
        assert nj == 1
        r = cos.shape[1]
        args += [cos, sin]
        in_specs += [pl.BlockSpec((tm, r), lambda i, j: (i + off, 0)),
                     pl.BlockSpec((tm, r), lambda i, j: (i + off, 0))]
        out_shape = jax.ShapeDtypeStruct((M, N // 2), out_dtype)
        out_specs = pl.BlockSpec((tm, N // 2), lambda i, j: (i, 0))
    else:
        raise ValueError(mode)
    if mode in ("norm", "resnorm") and nj > 1:
        if use_xn:
            scratch.append(pltpu.VMEM((tm, K), bf16))
        scratch.append(pltpu.VMEM((nj, tm, tn), f32))
    elif use_xn:
        scratch.append(pltpu.VMEM((tm, K), bf16))

    kern = functools.partial(_mm_kernel, prenorm=prenorm, dual=dual, na=na, mode=mode, nj=nj, tn=tn,
                             scale=scale, use_xn=use_xn, n_lat=n_lat)
    return pl.pallas_call(
        kern, grid=(ni, nj), in_specs=in_specs, out_specs=out_specs, out_shape=out_shape,
        scratch_shapes=scratch, compiler_params=_params("parallel", "arbitrary"), name=name,
    )(*args)


def _bmm_kernel(x_ref, w_ref, o_ref):
    o_ref[...] = jnp.dot(x_ref[...], w_ref[0], preferred_element_type=f32).astype(o_ref.dtype)


def head_matmul(x, w, *, out_dtype=bf16, name="head_matmul"):
    H, K, N = w.shape
    M = x.shape[0]
    return pl.pallas_call(
        _bmm_kernel, grid=(H,),
        in_specs=[pl.BlockSpec((M, K), lambda h: (0, h)), pl.BlockSpec((1, K, N), lambda h: (h, 0, 0))],
        out_specs=pl.BlockSpec((M, N), lambda h: (0, h)),
        out_shape=jax.ShapeDtypeStruct((M, H * N), out_dtype),
        compiler_params=_params("parallel"), name=name,
    )(x, w)


def _conv_silu(prev, cur, w_ref, b_ref, n_prev):
    L = cur.shape[0]
    xp = jnp.concatenate([prev, cur], axis=0)
    base = n_prev - (CONV_W - 1)
    acc = b_ref[...] + xp[base:base + L] * w_ref[0:1, :]
    for k in range(1, CONV_W):
        acc = acc + xp[base + k:base + k + L] * w_ref[k:k + 1, :]
    return _silu(acc)


def _gated_group_norm(y, z, g):
    v = y * _silu(z)
    return v * lax.rsqrt(jnp.mean(v * v, axis=-1, keepdims=True) + RMS_EPS) * g


def _expand_heads(v, e_ref):
    hi = v.astype(bf16)
    lo = (v - hi.astype(f32)).astype(bf16)
    e = e_ref[...]
    return jnp.dot(hi, e, preferred_element_type=f32) + jnp.dot(lo, e, preferred_element_type=f32)


def _ssd_prompt_kernel(z_ref, x_ref, bc_ref, dt_ref, cwx_ref, cwbc_ref, cbx_ref, cbbc_ref, dtb_ref, alog_ref,
                       dx_ref, ng_ref, e_ref, yg_ref, st_ref, xprev_ref, bcprev_ref, *, L, H, N, G):
    c = pl.program_id(1)
    n_prev = xprev_ref.shape[0]
    gw = G * N
    hpg = H // G
    cpg = x_ref.shape[1] // G
    P = cpg // hpg

    @pl.when(c == 0)
    def _():
        st_ref[...] = jnp.zeros_like(st_ref)
        xprev_ref[...] = jnp.zeros_like(xprev_ref)
        bcprev_ref[...] = jnp.zeros_like(bcprev_ref)

    xcur = x_ref[...]
    bccur = bc_ref[...]
    xc = _conv_silu(xprev_ref[...], xcur, cwx_ref, cbx_ref, n_prev)
    bcc = _conv_silu(bcprev_ref[...], bccur, cwbc_ref, cbbc_ref, n_prev)
    xprev_ref[...] = xcur[L - n_prev:]
    bcprev_ref[...] = bccur[L - n_prev:]

    dt = _softplus(dt_ref[...] + dtb_ref[...])
    a = dt * (-jnp.exp(alog_ref[...]))
    row = lax.broadcasted_iota(jnp.int32, (L, L), 0)
    col = lax.broadcasted_iota(jnp.int32, (L, L), 1)
    causal = row >= col
    acs = jnp.dot(causal.astype(f32), a, preferred_element_type=f32, precision=lax.Precision.HIGHEST)
    lane = lax.broadcasted_iota(jnp.int32, (L, LANES), 1)
    rows = jnp.where(lane < LANES // 2, acs, pltpu.roll(dt, LANES // 2, 1)).T
    acs_last = acs[L - 1:L, :]
    dec = jnp.exp(acs_last)
    eacs_x = _expand_heads(jnp.exp(acs), e_ref)
    seg_x = _expand_heads(jnp.exp(acs_last - acs) * dt, e_ref)
    xs = (xc * seg_x).astype(bf16)

    lo = lane < LANES // 2
    for g in range(G):
        sl = slice(g * cpg, (g + 1) * cpg)
        Bg = bcc[:, g * N:(g + 1) * N].astype(bf16)
        Cg = bcc[:, gw + g * N:gw + (g + 1) * N].astype(bf16)
        cb = lax.dot_general(Cg, Bg, _NT, preferred_element_type=f32)
        hg = st_ref[0, g]
        y = lax.dot_general(Cg, hg.astype(bf16), _NT, preferred_element_type=f32) * eacs_x[:, sl]
        y = y + xc[:, sl] * dx_ref[:, sl]
        ys = []
        for pp in range(hpg // 2):
            h0 = g * hpg + 2 * pp
            ms = []
            for e in (h0, h0 + 1):
                diff = acs[:, e:e + 1] - rows[e:e + 1, :]
                decay = jnp.exp(jnp.where(causal, diff, -jnp.inf))
                ms.append(cb * decay * rows[LANES // 2 + e:LANES // 2 + e + 1, :])
            xpair = xc[:, (h0 // 2) * LANES:(h0 // 2 + 1) * LANES]
            xbd = jnp.concatenate([jnp.where(lo, xpair, 0.0), jnp.where(lo, 0.0, xpair)], axis=0).astype(bf16)
            ys.append(jnp.dot(jnp.concatenate(ms, axis=1).astype(bf16), xbd, preferred_element_type=f32))
        y = y + jnp.concatenate(ys, axis=1)
        st = lax.dot_general(xs[:, sl], Bg, _TN, preferred_element_type=f32)
        decg = jnp.concatenate([jnp.broadcast_to(dec[:, h:h + 1], (P, N)) for h in range(g * hpg, (g + 1) * hpg)],
                               axis=0)
        st_ref[0, g] = decg * hg + st
        yg_ref[:, sl] = _gated_group_norm(y, z_ref[:, sl], ng_ref[:, sl]).astype(yg_ref.dtype)


def ssd_prompt(zx, dtp, cw, cb, dtb, alog, dx, ng, *, B, S, H, P, N, G):
    L = SSD_CHUNK
    nc = S // L
    di = H * P
    gw = 2 * G * N
    assert 2 * P == LANES and N == LANES and di % gw == 0 and H <= LANES // 2
    cwx, cwbc = cw[:, :di], cw[:, di:]
    cbx, cbbc = cb[:, :di], cb[:, di:]
    expand = (jnp.arange(LANES)[:, None] == jnp.arange(di)[None, :] // P).astype(bf16)
    kern = functools.partial(_ssd_prompt_kernel, L=L, H=H, N=N, G=G)
    const = lambda b, c: (0, 0)
    return pl.pallas_call(
        kern, grid=(B, nc),
        in_specs=[
            pl.BlockSpec((L, di), lambda b, c: (b * nc + c, 0)),
            pl.BlockSpec((L, di), lambda b, c: (b * nc + c, 1)),
            pl.BlockSpec((L, gw), lambda b, c: (b * nc + c, 2 * di // gw)),
            pl.BlockSpec((L, LANES), lambda b, c: (b * nc + c, 0)),
            pl.BlockSpec((CONV_W, di), const), pl.BlockSpec((CONV_W, gw), const),
            pl.BlockSpec((1, di), const), pl.BlockSpec((1, gw), const),
            pl.BlockSpec((1, LANES), const), pl.BlockSpec((1, LANES), const),
            pl.BlockSpec((1, di), const), pl.BlockSpec((1, di), const),
            pl.BlockSpec((LANES, di), const),
        ],
        out_specs=(pl.BlockSpec((L, di), lambda b, c: (b * nc + c, 0)),
                   pl.BlockSpec((1, G, di // G, N), lambda b, c: (b, 0, 0, 0))),
        out_shape=(jax.ShapeDtypeStruct((B * S, di), bf16), jax.ShapeDtypeStruct((B, G, di // G, N), f32)),
        scratch_shapes=[pltpu.VMEM((8, di), f32), pltpu.VMEM((8, gw), f32)],
        compiler_params=_params("parallel", "arbitrary"), name="ssd_prompt",
    )(zx, zx, zx, dtp, cwx, cwbc, cbx, cbbc, dtb, alog, dx, ng, expand)


def _ssd_sample_kernel(z_ref, x_ref, bc_ref, dtx_ref, cpx_ref, cpbc_ref, h0_ref, cwx_ref, cwbc_ref, cbx_ref,
                       cbbc_ref, dtb_ref, alog_ref, dx_ref, ng_ref, yg_ref, st_ref, *, L, N, G):
    di = x_ref.shape[1]
    gw = G * N
    cpg = di // G
    xc = _conv_silu(cpx_ref[0], x_ref[...], cwx_ref, cbx_ref, CONV_W - 1)
    bcc = _conv_silu(cpbc_ref[0], bc_ref[...], cwbc_ref, cbbc_ref, CONV_W - 1)
    dt = _softplus(dtx_ref[...] + dtb_ref[...])
    a = dt * (-jnp.exp(alog_ref[...]))
    row = lax.broadcasted_iota(jnp.int32, (L, di), 0)
    acs = jnp.zeros((L, di), f32)
    for s in range(L):
        acs = acs + jnp.where(row >= s, jnp.broadcast_to(a[s:s + 1, :], (L, di)), 0.0)
    acs_last = acs[L - 1:L, :]
    xdt = xc * dt

    cbx = []
    yoff = []
    for g in range(G):
        Bg = bcc[:, g * N:(g + 1) * N].astype(bf16)
        Cg = bcc[:, gw + g * N:gw + (g + 1) * N].astype(bf16)
        cb = lax.dot_general(Cg, Bg, _NT, preferred_element_type=f32)
        cbx.append([jnp.broadcast_to(cb[:, s:s + 1], (L, cpg)) for s in range(L)])
        hg = h0_ref[0, g]
        yoff.append(lax.dot_general(Cg, hg.astype(bf16), _NT, preferred_element_type=f32))
        sl = slice(g * cpg, (g + 1) * cpg)
        xd = (xdt[:, sl] * jnp.exp(acs_last[:, sl] - acs[:, sl])).astype(bf16)
        st = lax.dot_general(xd, Bg, _TN, preferred_element_type=f32)
        dcol = jnp.exp(jnp.broadcast_to(acs_last[:, sl], (L, cpg)).T[:, 0:1])
        st_ref[0, g] = dcol * hg + st
    y = jnp.exp(acs) * jnp.concatenate(yoff, axis=1) + xc * dx_ref[...]
    for s in range(L):
        decay = jnp.exp(jnp.where(row >= s, acs - jnp.broadcast_to(acs[s:s + 1, :], (L, di)), -jnp.inf))
        cbs = jnp.concatenate([cbx[g][s] for g in range(G)], axis=1)
        y = y + cbs * decay * jnp.broadcast_to(xdt[s:s + 1, :], (L, di))
    for g in range(G):
        sl = slice(g * cpg, (g + 1) * cpg)
        yg_ref[:, sl] = _gated_group_norm(y[:, sl], z_ref[:, sl], ng_ref[:, sl]).astype(yg_ref.dtype)


def ssd_sample(zx, dtx, conv_prev, h0, cw, cb, dtbx, alogx, dx, ng, *, B, L, row0, G, N):
    di = dtx.shape[1]
    gw = 2 * G * N
    cpg = di // G
    cwx, cwbc = cw[:, :di], cw[:, di:]
    cbx, cbbc = cb[:, :di], cb[:, di:]
    rb = row0 // L
    kern = functools.partial(_ssd_sample_kernel, L=L, N=N, G=G)
    const = lambda b: (0, 0)
    return pl.pallas_call(
        kern, grid=(B,),
        in_specs=[
            pl.BlockSpec((L, di), lambda b: (rb + b, 0)),
            pl.BlockSpec((L, di), lambda b: (rb + b, 1)),
            pl.BlockSpec((L, gw), lambda b: (rb + b, 2 * di // gw)),
            pl.BlockSpec((L, di), lambda b: (b, 0)),
            pl.BlockSpec((1, CONV_W - 1, di), lambda b: (b, 0, 0)),
            pl.BlockSpec((1, CONV_W - 1, gw), lambda b: (b, 0, di // gw)),
            pl.BlockSpec((1, G, cpg, N), lambda b: (b, 0, 0, 0)),
            pl.BlockSpec((CONV_W, di), const), pl.BlockSpec((CONV_W, gw), const),
            pl.BlockSpec((1, di), const), pl.BlockSpec((1, gw), const),
            pl.BlockSpec((1, di), const), pl.BlockSpec((1, di), const),
            pl.BlockSpec((1, di), const), pl.BlockSpec((1, di), const),
        ],
        out_specs=(pl.BlockSpec((L, di), lambda b: (b, 0)),
                   pl.BlockSpec((1, G, cpg, N), lambda b: (b, 0, 0, 0))),
        out_shape=(jax.ShapeDtypeStruct((B * L, di), f32), jax.ShapeDtypeStruct(h0.shape, f32)),
        compiler_params=_params("parallel"), name="ssd_sample",
    )(zx, zx, zx, dtx, conv_prev, conv_prev, h0, cwx, cwbc, cbx, cbbc, dtbx, alogx, dx, ng)


def _attn_prompt_kernel(qn_ref, qr_ref, kn_ref, v_ref, kr_ref, o_ref, kf_ref, *, tq):
    qi = pl.program_id(2)
    dn, dr, dk, dv = kn_ref.shape[1], kr_ref.shape[1], kf_ref.shape[1], v_ref.shape[1]

    @pl.when(qi == 0)
    def _():
        kf_ref[:, :dn] = kn_ref[...]
        kf_ref[:, dn:dn + dr] = kr_ref[...]
        kf_ref[:, dn + dr:] = jnp.zeros((kf_ref.shape[0], dk - dn - dr), bf16)

    q = jnp.concatenate([qn_ref[...], qr_ref[0], jnp.zeros((tq, dk - dn - dr), bf16)], axis=1)

    def block(carry, k0, diagonal=False):
        ks = pl.ds(pl.multiple_of(k0, tq), tq)
        s = lax.dot_general(q, kf_ref[ks, :], _NT, preferred_element_type=f32)
        if diagonal:
            r = lax.broadcasted_iota(jnp.int32, (tq, tq), 0)
            c = lax.broadcasted_iota(jnp.int32, (tq, tq), 1)
            s = jnp.where(r >= c, s, -jnp.inf)
        m, l, acc = carry
        m_new = jnp.maximum(m, jnp.max(s, axis=-1, keepdims=True))
        alpha = jnp.exp(m - m_new)
        p = jnp.exp(s - m_new)
        l = l * alpha + jnp.sum(p, axis=-1, keepdims=True)
        acc = acc * alpha + jnp.dot(p.astype(bf16), v_ref[ks, :], preferred_element_type=f32)
        return m_new, l, acc

    def pair(i, carry):
        return block(block(carry, 2 * i * tq), (2 * i + 1) * tq)

    carry = (jnp.full((tq, 1), -jnp.inf, f32), jnp.zeros((tq, 1), f32), jnp.zeros((tq, dv), f32))
    carry = lax.fori_loop(0, qi // 2, pair, carry)
    carry = lax.cond(qi % 2 == 1, lambda c: block(c, (qi - 1) * tq), lambda c: c, carry)
    _, l, acc = block(carry, qi * tq, diagonal=True)
    o_ref[...] = (acc / l).astype(o_ref.dtype)


def attn_prompt(qn, qr_hm, kv, krb, *, B, S, H, dn, dv, tq):
    nq = S // tq
    dr = krb.shape[1]
    assert dn == dv
    return pl.pallas_call(
        functools.partial(_attn_prompt_kernel, tq=tq), grid=(B, H, nq),
        in_specs=[
            pl.BlockSpec((tq, dn), lambda b, h, i: (b * nq + i, h)),
            pl.BlockSpec((1, tq, dr), lambda b, h, i: (h, b * nq + i, 0)),
            pl.BlockSpec((S, dn), lambda b, h, i: (b, h)),
            pl.BlockSpec((S, dv), lambda b, h, i: (b, H + h)),
            pl.BlockSpec((S, dr), lambda b, h, i: (b, 0)),
        ],
        out_specs=pl.BlockSpec((tq, dv), lambda b, h, i: (b * nq + i, h)),
        out_shape=jax.ShapeDtypeStruct((B * S, H * dv), bf16),
        scratch_shapes=[pltpu.VMEM((S, 2 * LANES), bf16)],
        compiler_params=_params("parallel", "parallel", "arbitrary"), name="attn_prompt",
    )(qn, qr_hm, kv, kv, krb)


def _attn_sample_kernel(pt_ref, ql_ref, qr_ref, cn_ref, krn_ref, lat_hbm, krt_hbm, o_ref,
                        latbuf, krtbuf, lat_sem, krt_sem, cb_ref, kt_ref, *, n_seq, n_pages, G, P, H, sub):
    b = pl.program_id(0)
    ql = ql_ref[0]
    qr = qr_ref[0]
    R = ql.shape[0]
    pg = latbuf.shape[2]

    def group_copies(seq, g):
        slot = g % 2
        out = []
        for k in range(P):
            page = pt_ref[seq * n_pages + g * P + k]
            out.append(pltpu.make_async_copy(lat_hbm.at[page], latbuf.at[slot, k], lat_sem.at[slot]))
            out.append(pltpu.make_async_copy(krt_hbm.at[page], krtbuf.at[slot, k], krt_sem.at[slot]))
        return out

    @pl.when(b == 0)
    def _():
        for cp in group_copies(0, 0):
            cp.start()

    cn = cn_ref[0].astype(bf16)
    S = cn.shape[0]
    s = (lax.dot_general(ql, cn, _NT, preferred_element_type=f32)
         + lax.dot_general(qr, krn_ref[0].astype(bf16), _NT, preferred_element_type=f32))
    qpos = lax.broadcasted_iota(jnp.int32, (R, S), 0) // H
    kpos = lax.broadcasted_iota(jnp.int32, (R, S), 1)
    s = jnp.where(kpos <= qpos, s, -jnp.inf)
    m = jnp.max(s, axis=-1, keepdims=True)
    p = jnp.exp(s - m)
    l = jnp.sum(p, axis=-1, keepdims=True)
    acc = jnp.dot(p.astype(bf16), cn, preferred_element_type=f32)

    nsb = P // sub
    w = sub * pg
    for g in range(G):
        slot = g % 2
        if g + 1 < G:
            for cp in group_copies(b, g + 1):
                cp.start()
        else:
            @pl.when(b + 1 < n_seq)
            def _():
                for cp in group_copies(b + 1, 0):
                    cp.start()
        for cp in group_copies(b, g):
            cp.wait()
        ss = []
        for sb in range(nsb):
            for k in range(sb * sub, (sb + 1) * sub):
                cb_ref[k * pg:(k + 1) * pg, :] = latbuf[slot, k].astype(bf16)
                kt_ref[:, k * pg:(k + 1) * pg] = krtbuf[slot, k].astype(bf16)
            ss.append(lax.dot_general(ql, cb_ref[sb * w:(sb + 1) * w, :], _NT, preferred_element_type=f32)
                      + jnp.dot(qr, kt_ref[:, sb * w:(sb + 1) * w], preferred_element_type=f32))
        m_new = m
        for s in ss:
            m_new = jnp.maximum(m_new, jnp.max(s, axis=-1, keepdims=True))
        alpha = jnp.exp(m - m_new)
        m = m_new
        l = l * alpha
        acc = acc * alpha
        for sb, s in enumerate(ss):
            p = jnp.exp(s - m_new)
            l = l + jnp.sum(p, axis=-1, keepdims=True)
            acc = acc + jnp.dot(p.astype(bf16), cb_ref[sb * w:(sb + 1) * w, :], preferred_element_type=f32)
    o_ref[0] = (acc / l).astype(o_ref.dtype)


def attn_sample(page_table, ql, qr, c_new, kr_new, cache_lat, cache_krt, *, H, pages_per_group):
    B, R, C = ql.shape
    dr = qr.shape[2]
    S = c_new.shape[1]
    n_pages = page_table.shape[1]
    P = pages_per_group
    G = n_pages // P
    assert G * P == n_pages and G % 2 == 0, "page groups alternate between two buffer slots"
    pg = cache_lat.shape[1]
    seq = lambda b, pt_ref: (b, 0, 0)
    kern = functools.partial(_attn_sample_kernel, n_seq=B, n_pages=n_pages, G=G, P=P, H=H,
                             sub=min(P, SAMPLE_SUB_PAGES))
    return pl.pallas_call(
        kern,
        grid_spec=pltpu.PrefetchScalarGridSpec(
            num_scalar_prefetch=1, grid=(B,),
            in_specs=[pl.BlockSpec((1, R, C), seq), pl.BlockSpec((1, R, dr), seq),
                      pl.BlockSpec((1, S, C), seq), pl.BlockSpec((1, S, dr), seq),
                      pl.BlockSpec(memory_space=pl.ANY), pl.BlockSpec(memory_space=pl.ANY)],
            out_specs=pl.BlockSpec((1, R, C), seq),
            scratch_shapes=[pltpu.VMEM((2, P, pg, C), cache_lat.dtype), pltpu.VMEM((2, P, dr, pg), cache_krt.dtype),
                            pltpu.SemaphoreType.DMA((2,)), pltpu.SemaphoreType.DMA((2,)),
                            pltpu.VMEM((P * pg, C), bf16), pltpu.VMEM((dr, P * pg), bf16)]),
        out_shape=jax.ShapeDtypeStruct((B, R, C), bf16),
        compiler_params=_params("arbitrary"), name="attn_sample",
    )(page_table.reshape(-1), ql, qr, c_new, kr_new, cache_lat, cache_krt)


def _rope_tables(pos, half):
    inv = ROPE_THETA ** (-jnp.arange(half, dtype=f32) / half)
    ang = pos[:, None] * inv[None, :]
    cos, sin = jnp.cos(ang), jnp.sin(ang)
    return jnp.concatenate([cos, cos], axis=1), jnp.concatenate([-sin, sin], axis=1)


def _swap_halves(w, r):
    k, n = w.shape
    return w.reshape(k, n // r, 2, r // 2)[:, :, ::-1, :].reshape(k, n)


def kernel(x_prompt, x_sample, cache_kv_latent, cache_k_rope, state_ssm, state_conv, page_table, ffn_pre_g, ffn_post_g, ffn_w_gate, ffn_w_up, ffn_w_down, mix_pre_g, mix_post_g, m_in_proj, m_conv_w, m_conv_b, m_dt_bias, m_A_log, m_D, m_norm_g, m_out_proj, kv_norm_g, w_dkv, kv_latent_norm_g, w_uk, w_uv, q_w_dq, q_norm_g, q_w_uq, attn_w_o):
    Bp, S, D = x_prompt.shape
    Bs, Ls, _ = x_sample.shape
    Tp, Ts = Bp * S, Bs * Ls
    T = Tp + Ts
    depth = ffn_pre_g.shape[0]
    n_a = state_ssm.shape[0]
    assert depth == 2 and n_a == 1, "layer pattern: one SSD layer then one MLA layer"
    H, P, N = state_ssm.shape[2:]
    G = SSM_GROUPS
    di = H * P
    conv_dim = state_conv.shape[3]
    C = cache_kv_latent.shape[2]
    dr = cache_k_rope.shape[2]
    Hm, dn = w_uk.shape[1:]
    dv = w_uv.shape[2]
    past_len = page_table.shape[1] * cache_kv_latent.shape[1]
    scale = float(dn + dr) ** -0.5
    tm = 512
    row = lambda v: v.reshape(1, -1)

    wg_all, wu_all, wd_all = ffn_w_gate.astype(bf16), ffn_w_up.astype(bf16), ffn_w_down.astype(bf16)

    def ffn(h, layer, j):
        return ffn_half(h, row(ffn_pre_g[layer, j]), wg_all, wu_all, wd_all, row(ffn_post_g[layer, j]),
                        sel=(layer, j), tm=tm, tf=512)

    pos = jnp.concatenate([jnp.tile(jnp.arange(S, dtype=f32), Bp),
                           jnp.tile(past_len + jnp.arange(Ls, dtype=f32), Bs)])
    cos64, sin64 = _rope_tables(pos, dr // 2)
    cos128, sin128 = jnp.tile(cos64, (1, 2)), jnp.tile(sin64, (1, 2))

    h = jnp.concatenate([x_prompt.reshape(Tp, D), x_sample.reshape(Ts, D)], axis=0)

    h = ffn(h, 0, 0)
    w_in = m_in_proj[0]
    nzx = di + conv_dim
    w_zx = w_in[:, :nzx].astype(bf16)
    w_dt = w_in[:, nzx:]
    w_dtp = jnp.pad(w_dt, ((0, 0), (0, LANES - H))).astype(bf16)
    w_dtx = jnp.repeat(w_dt, P, axis=1).astype(bf16)
    g_mix0 = row(mix_pre_g[0])
    zx = fused_matmul(h, w_zx, tm=2 * tm, tn=1024, g_pre=g_mix0, name="in_proj_zx")
    dtp = fused_matmul(h, w_dtp, tm=tm, g_pre=g_mix0, n_rows=Tp, name="in_proj_dt")
    dtx = fused_matmul(h, w_dtx, tm=tm, tn=1024, g_pre=g_mix0, n_rows=Ts, row_block_offset=Tp // tm,
                       name="in_proj_dtx")
    cw, cb = m_conv_w[0], row(m_conv_b[0])
    dx = row(jnp.repeat(m_D[0], P))
    ng = row(m_norm_g[0])
    pad_h = lambda v: row(jnp.pad(v, (0, LANES - H)))
    yg_p, ssm_p = ssd_prompt(zx, dtp, cw, cb, pad_h(m_dt_bias[0]), pad_h(m_A_log[0]), dx, ng,
                             B=Bp, S=S, H=H, P=P, N=N, G=G)
    yg_s, ssm_s = ssd_sample(zx, dtx, state_conv[0], state_ssm[0].reshape(Bs, G, di // G, N), cw, cb,
                             row(jnp.repeat(m_dt_bias[0], P)), row(jnp.repeat(m_A_log[0], P)), dx, ng,
                             B=Bs, L=Ls, row0=Tp, G=G, N=N)
    h = fused_matmul(yg_p, m_out_proj[0].astype(bf16), xb=yg_s.astype(bf16), tm=tm, tn=512, mode="resnorm", res=h,
                     g_post=row(mix_post_g[0]), name="out_proj")
    h = ffn(h, 0, 1)

    tail = CONV_W - 1
    conv_prompt = jnp.stack([lax.slice(zx, ((b + 1) * S - tail, di), ((b + 1) * S, nzx)) for b in range(Bp)])[None]
    conv_sample = zx[Tp:].reshape(Bs, Ls, -1)[:, Ls - tail:, di:][None]
    ssm_prompt = ssm_p.reshape(1, Bp, H, P, N)
    ssm_sample = ssm_s.reshape(1, Bs, H, P, N)

    w_kv = jnp.concatenate([w_dkv, _swap_halves(w_dkv[:, C:], dr)], axis=1).astype(bf16)
    lat, kr, krb = fused_matmul(h, w_kv, tm=tm, mode="kv", g_pre=row(kv_norm_g), g_post=row(kv_latent_norm_g),
                                cos=cos64, sin=sin64, n_lat=C, name="shared_kv")

    h = ffn(h, 1, 0)
    cq = fused_matmul(h, q_w_dq[0].astype(bf16), tm=tm, mode="norm", g_pre=row(mix_pre_g[1]),
                      g_post=row(q_norm_g[0]), out_dtype=bf16, name="q_down")
    w_uq = q_w_uq[0]
    qrank = w_uq.shape[0]
    w_qn = w_uq[:, :, :dn].reshape(qrank, Hm * dn).astype(bf16)
    w_qr = w_uq[:, :, dn:].reshape(qrank, Hm * dr)
    w_qr2 = jnp.concatenate([w_qr, _swap_halves(w_qr, dr)], axis=1).astype(bf16)
    qn = fused_matmul(cq, w_qn, tm=tm, scale=scale, out_dtype=bf16, name="q_nope")
    qr = fused_matmul(cq, w_qr2, tm=tm, mode="rope", cos=cos128, sin=sin128, scale=scale, out_dtype=bf16,
                      name="q_rope")

    w_kvup = jnp.concatenate([w_uk.reshape(C, Hm * dn), w_uv.reshape(C, Hm * dv)], axis=1).astype(bf16)
    kv = fused_matmul(lat, w_kvup, tm=tm, tn=1024, n_rows=Tp, out_dtype=bf16, name="kv_up")
    qr_hm = qr.reshape(T, Hm, dr).transpose(1, 0, 2)
    o_p = attn_prompt(qn, qr_hm, kv, krb, B=Bp, S=S, H=Hm, dn=dn, dv=dv, tq=512)

    w_ukh = w_uk.transpose(1, 2, 0).astype(bf16)
    w_uvh = w_uv.transpose(1, 0, 2).astype(bf16)
    ql = head_matmul(qn[Tp:], w_ukh, name="q_absorb")
    o_lat = attn_sample(page_table, ql.reshape(Bs, Ls * Hm, C), qr[Tp:].reshape(Bs, Ls * Hm, dr),
                        lat[Tp:].reshape(Bs, Ls, C), kr[Tp:].reshape(Bs, Ls, dr),
                        cache_kv_latent, cache_k_rope.transpose(0, 2, 1), H=Hm,
                        pages_per_group=min(32, page_table.shape[1] // 2))
    o_s = head_matmul(o_lat.reshape(Ts, Hm * C), w_uvh, name="v_up")

    h = fused_matmul(o_p, attn_w_o[0].reshape(Hm * dv, D).astype(bf16), xb=o_s, tm=tm, tn=512, mode="resnorm",
                     res=h, g_post=row(mix_post_g[1]), name="attn_out")
    h = ffn(h, 1, 1)

    y_prompt = h[:Tp].reshape(Bp, S, D)
    y_sample = h[Tp:].reshape(Bs, Ls, D)
    return (y_prompt, y_sample, lat[:Tp].reshape(Bp, S, C), kr[:Tp].reshape(Bp, S, dr), ssm_prompt, conv_prompt,
            lat[Tp:].reshape(Bs, Ls, C), kr[Tp:].reshape(Bs, Ls, dr), ssm_sample, conv_sample)
```

```python
import functools

import jax
import jax.numpy as jnp
from jax import lax
from jax.experimental import pallas as pl
from jax.experimental.pallas import tpu as pltpu

f32 = jnp.float32
bf16 = jnp.bfloat16

RMS_EPS = 1e-6
ROPE_THETA = 10000.0
SSD_CHUNK = 128
SSM_GROUPS = 8
CONV_W = 4
SAMPLE_SUB_PAGES = 8
LANES = 128
V7X_VMEM_BYTES = 64 * 1024 * 1024
VMEM_LIMIT = V7X_VMEM_BYTES - 8 * 1024 * 1024

_NT = (((1,), (1,)), ((), ()))
_TN = (((0,), (0,)), ((), ()))


def _rms(x, g):
    return x * lax.rsqrt(jnp.mean(x * x, axis=-1, keepdims=True) + RMS_EPS) * g


def _silu(x):
    return x * jax.nn.sigmoid(x)


def _softplus(x):
    return jnp.maximum(x, 0.0) + jnp.log1p(jnp.exp(-jnp.abs(x)))


def _params(*sem):
    return pltpu.CompilerParams(dimension_semantics=sem, vmem_limit_bytes=VMEM_LIMIT)


def _ffn_kernel(*refs, nj, dchunk, na, dual_in, dual_out):
    refs = list(refs)
    h_refs = [refs.pop(0) for _ in range(2 if dual_in else 1)]
    gpre_ref, wg_ref, wu_ref, wd_ref, gpost_ref = refs[:5]
    o_refs = refs[5:7] if dual_out else refs[5:6]
    xn_ref, acc_ref = refs[-2:]
    i = pl.program_id(0)
    j = pl.program_id(1)
    sides = [(i < na, h_refs[0], o_refs[0]), (i >= na, h_refs[-1], o_refs[-1])] if dual_in or dual_out \
        else [(True, h_refs[0], o_refs[0])]

    for cond, h_ref, _ in sides:
        @pl.when((j == 0) & cond)
        def _(h_ref=h_ref):
            xn_ref[...] = _rms(h_ref[...], gpre_ref[...]).astype(bf16)
            acc_ref[...] = jnp.zeros_like(acc_ref)

    xn = xn_ref[...]
    g = jnp.dot(xn, wg_ref[...], preferred_element_type=f32)
    u = jnp.dot(xn, wu_ref[...], preferred_element_type=f32)
    a = (_silu(g) * u).astype(bf16)
    for c in range(0, acc_ref.shape[1], dchunk):
        acc_ref[:, c:c + dchunk] += jnp.dot(a, wd_ref[:, c:c + dchunk], preferred_element_type=f32)

    for cond, h_ref, o_ref in sides:
        @pl.when((j == nj - 1) & cond)
        def _(h_ref=h_ref, o_ref=o_ref):
            o_ref[...] = h_ref[...] + 0.5 * _rms(acc_ref[...], gpost_ref[...])


def ffn_half(h, g_pre, wg, wu, wd, g_post, *, sel, tm, tf, hb=None, split_out=None):
    D = h.shape[1]
    F = wg.shape[3]
    nj = F // tf
    T = h.shape[0] + (hb.shape[0] if hb is not None else 0)
    dual_in, dual_out = hb is not None, split_out is not None
    n_first = h.shape[0] if dual_in else (split_out if dual_out else T)
    na = n_first // tm
    assert na * tm == n_first and T % tm == 0
    first = lambda i, j: (jnp.minimum(i, na - 1), 0)
    second = lambda i, j: (jnp.maximum(i - na, 0), 0)
    whole = lambda i, j: (i, 0)
    const = lambda i, j: (0, 0)
    h_args = [h, hb] if dual_in else [h]
    h_specs = [pl.BlockSpec((tm, D), first), pl.BlockSpec((tm, D), second)] if dual_in else [pl.BlockSpec((tm, D), whole)]
    if dual_out:
        out_specs = (pl.BlockSpec((tm, D), first), pl.BlockSpec((tm, D), second))
        out_shape = (jax.ShapeDtypeStruct((n_first, D), f32), jax.ShapeDtypeStruct((T - n_first, D), f32))
    else:
        out_specs = pl.BlockSpec((tm, D), whole)
        out_shape = jax.ShapeDtypeStruct((T, D), f32)
    return pl.pallas_call(
        functools.partial(_ffn_kernel, nj=nj, dchunk=min(D, 512), na=na, dual_in=dual_in, dual_out=dual_out),
        grid=(T // tm, nj),
        in_specs=h_specs + [
            pl.BlockSpec((1, D), const),
            pl.BlockSpec((None, None, D, tf), lambda i, j: (*sel, 0, j)),
            pl.BlockSpec((None, None, D, tf), lambda i, j: (*sel, 0, j)),
            pl.BlockSpec((None, None, tf, D), lambda i, j: (*sel, j, 0)),
            pl.BlockSpec((1, D), const),
        ],
        out_specs=out_specs,
        out_shape=out_shape,
        scratch_shapes=[pltpu.VMEM((tm, D), bf16), pltpu.VMEM((tm, D), f32)],
        compiler_params=_params("arbitrary" if dual_out else "parallel", "arbitrary"),
        name="ffn_half",
    )(*h_args, g_pre, wg, wu, wd, g_post)


def _mm_kernel(*refs, prenorm, dual, na, mode, nj, tn, scale, use_xn, n_lat):
    refs = list(refs)
    xa_ref = refs.pop(0)
    xb_ref = refs.pop(0) if dual else None
    gpre_ref = refs.pop(0) if prenorm else None
    w_ref = refs.pop(0)
    i = pl.program_id(0)
    j = pl.program_id(1)

    def load_x(x_ref):
        x = x_ref[...]
        if prenorm:
            x = _rms(x.astype(f32), gpre_ref[...])
        return x.astype(bf16)

    if use_xn:
        xn_ref = refs[-1] if mode in ("plain", "kv", "rope") or nj == 1 else refs[-2]

        if dual:
            @pl.when((j == 0) & (i < na))
            def _():
                xn_ref[...] = load_x(xa_ref)

            @pl.when((j == 0) & (i >= na))
            def _():
                xn_ref[...] = load_x(xb_ref)
        else:
            @pl.when(j == 0)
            def _():
                xn_ref[...] = load_x(xa_ref)

        xn = xn_ref[...]
    else:
        xn = xa_ref[...]
    acc = jnp.dot(xn, w_ref[...], preferred_element_type=f32)

    if mode == "plain":
        o_ref = refs[0]
        o_ref[...] = (acc * scale if scale != 1.0 else acc).astype(o_ref.dtype)
    elif mode in ("norm", "resnorm"):
        if mode == "resnorm":
            res_ref, gpost_ref, o_ref = refs[0], refs[1], refs[2]
        else:
            res_ref, gpost_ref, o_ref = None, refs[0], refs[1]

        def finish(slabs):
            n = nj * tn
            ssq = sum(jnp.sum(s * s, axis=-1, keepdims=True) for s in slabs)
            inv = lax.rsqrt(ssq * (1.0 / n) + RMS_EPS)
            for k, s in enumerate(slabs):
                y = s * inv * gpost_ref[:, k * tn:(k + 1) * tn]
                if res_ref is not None:
                    y = res_ref[:, k * tn:(k + 1) * tn] + y
                o_ref[:, k * tn:(k + 1) * tn] = y.astype(o_ref.dtype)

        if nj == 1:
            finish([acc])
        else:
            slab_ref = refs[-1]
            slab_ref[j] = acc

            @pl.when(j == nj - 1)
            def _():
                finish([slab_ref[k] for k in range(nj)])
    elif mode == "kv":
        gpost_ref, cos_ref, sin_ref, lat_ref, kr_ref, krb_ref = refs[:6]
        lat_ref[...] = _rms(acc[:, :n_lat], gpost_ref[...])
        r = cos_ref.shape[1]
        kr = acc[:, n_lat:n_lat + r] * cos_ref[...] + acc[:, n_lat + r:n_lat + 2 * r] * sin_ref[...]
        kr_ref[...] = kr
        krb_ref[...] = kr.astype(bf16)
    elif mode == "rope":
        cos_ref, sin_ref, o_ref = refs[:3]
        half = acc.shape[1] // 2
        reps = half // cos_ref.shape[1]
        cos = jnp.concatenate([cos_ref[...]] * reps, axis=1)
        sin = jnp.concatenate([sin_ref[...]] * reps, axis=1)
        o_ref[...] = ((acc[:, :half] * cos + acc[:, half:] * sin) * scale).astype(o_ref.dtype)
    else:
        raise ValueError(mode)


def fused_matmul(x, w, *, tm, tn=None, mode="plain", g_pre=None, xb=None, res=None, g_post=None,
                 cos=None, sin=None, scale=1.0, out_dtype=f32, n_rows=None, row_block_offset=0,
                 n_lat=0, name="fused_matmul"):
    K, N = w.shape
    if tn is None:
        tn = N
    nj = N // tn
    dual = xb is not None
    na = x.shape[0] // tm
    M = (x.shape[0] + (xb.shape[0] if dual else 0)) if n_rows is None else n_rows
    ni = M // tm
    off = row_block_offset
    prenorm = g_pre is not None
    use_xn = prenorm or dual or x.dtype != bf16

    args, in_specs = [], []
    if dual:
        args += [x, xb]
        in_specs += [pl.BlockSpec((tm, K), lambda i, j: (jnp.minimum(i, na - 1), 0)),
                     pl.BlockSpec((tm, K), lambda i, j: (jnp.maximum(i - na, 0), 0))]
    else:
        args.append(x)
        in_specs.append(pl.BlockSpec((tm, K), lambda i, j: (i + off, 0)))
    if prenorm:
        args.append(g_pre)
        in_specs.append(pl.BlockSpec((1, K), lambda i, j: (0, 0)))
    args.append(w)
    in_specs.append(pl.BlockSpec((K, tn), lambda i, j: (0, j)))

    scratch = []
    if mode == "plain":
        out_shape = jax.ShapeDtypeStruct((M, N), out_dtype)
        out_specs = pl.BlockSpec((tm, tn), lambda i, j: (i, j))
    elif mode in ("norm", "resnorm"):
        if mode == "resnorm":
            args.append(res)
            in_specs.append(pl.BlockSpec((tm, N), lambda i, j: (i + off, 0)))
        args.append(g_post)
        in_specs.append(pl.BlockSpec((1, N), lambda i, j: (0, 0)))
        out_shape = jax.ShapeDtypeStruct((M, N), out_dtype)
        out_specs = pl.BlockSpec((tm, N), lambda i, j: (i, 0))
    elif mode == "kv":
        assert nj == 1
        r = cos.shape[1]
        args += [g_post, cos, sin]
        in_specs += [pl.BlockSpec((1, n_lat), lambda i, j: (0, 0)),
                     pl.BlockSpec((tm, r), lambda i, j: (i + off, 0)),
                     pl.BlockSpec((tm, r), lambda i, j: (i + off, 0))]
        out_shape = (jax.ShapeDtypeStruct((M, n_lat), f32), jax.ShapeDtypeStruct((M, r), f32),
                     jax.ShapeDtypeStruct((M, r), bf16))
        out_specs = (pl.BlockSpec((tm, n_lat), lambda i, j: (i, 0)), pl.BlockSpec((tm, r), lambda i, j: (i, 0)),
                     pl.BlockSpec((tm, r), lambda i, j: (i, 0)))
    elif mode == "rope":
        assert nj == 1
        r = cos.shape[1]
        args += [cos, sin]
        in_specs += [pl.BlockSpec((tm, r), lambda i, j: (i + off, 0)),
                     pl.BlockSpec((tm, r), lambda i, j: (i + off, 0))]
        out_shape = jax.ShapeDtypeStruct((M, N // 2), out_dtype)
        out_specs = pl.BlockSpec((tm, N // 2), lambda i, j: (i, 0))
    else:
        raise ValueError(mode)
    if mode in ("norm", "resnorm") and nj > 1:
        if use_xn:
            scratch.append(pltpu.VMEM((tm, K), bf16))
        scratch.append(pltpu.VMEM((nj, tm, tn), f32))
    elif use_xn:
        scratch.append(pltpu.VMEM((tm, K), bf16))

    kern = functools.partial(_mm_kernel, prenorm=prenorm, dual=dual, na=na, mode=mode, nj=nj, tn=tn,
                             scale=scale, use_xn=use_xn, n_lat=n_lat)
    return pl.pallas_call(
        kern, grid=(ni, nj), in_specs=in_specs, out_specs=out_specs, out_shape=out_shape,
        scratch_shapes=scratch, compiler_params=_params("parallel", "arbitrary"), name=name,
    )(*args)


def _bmm_kernel(x_ref, w_ref, o_ref):
    o_ref[...] = jnp.dot(x_ref[...], w_ref[0], preferred_element_type=f32).astype(o_ref.dtype)


def head_matmul(x, w, *, out_dtype=bf16, name="head_matmul"):
    H, K, N = w.shape
    M = x.shape[0]
    return pl.pallas_call(
        _bmm_kernel, grid=(H,),
        in_specs=[pl.BlockSpec((M, K), lambda h: (0, h)), pl.BlockSpec((1, K, N), lambda h: (h, 0, 0))],
        out_specs=pl.BlockSpec((M, N), lambda h: (0, h)),
        out_shape=jax.ShapeDtypeStruct((M, H * N), out_dtype),
        compiler_params=_params("parallel"), name=name,
    )(x, w)


def _conv_silu(prev, cur, w_ref, b_ref, n_prev):
    L = cur.shape[0]
    xp = jnp.concatenate([prev, cur], axis=0)
    base = n_prev - (CONV_W - 1)
    acc = b_ref[...] + xp[base:base + L] * w_ref[0:1, :]
    for k in range(1, CONV_W):
        acc = acc + xp[base + k:base + k + L] * w_ref[k:k + 1, :]
    return _silu(acc)


def _gated_group_norm(y, z, g):
    v = y * _silu(z)
    return v * lax.rsqrt(jnp.mean(v * v, axis=-1, keepdims=True) + RMS_EPS) * g


def _expand_heads(v, e_ref):
    hi = v.astype(bf16)
    lo = (v - hi.astype(f32)).astype(bf16)
    e = e_ref[...]
    return jnp.dot(hi, e, preferred_element_type=f32) + jnp.dot(lo, e, preferred_element_type=f32)


def _ssd_prompt_kernel(z_ref, x_ref, bc_ref, dt_ref, cwx_ref, cwbc_ref, cbx_ref, cbbc_ref, dtb_ref, alog_ref,
                       dx_ref, ng_ref, e_ref, yg_ref, st_ref, xprev_ref, bcprev_ref, *, L, H, N, G):
    c = pl.program_id(1)
    n_prev = xprev_ref.shape[0]
    gw = G * N
    hpg = H // G
    cpg = x_ref.shape[1] // G
    P = cpg // hpg

    @pl.when(c == 0)
    def _():
        st_ref[...] = jnp.zeros_like(st_ref)
        xprev_ref[...] = jnp.zeros_like(xprev_ref)
        bcprev_ref[...] = jnp.zeros_like(bcprev_ref)

    xcur = x_ref[...]
    bccur = bc_ref[...]
    xc = _conv_silu(xprev_ref[...], xcur, cwx_ref, cbx_ref, n_prev)
    bcc = _conv_silu(bcprev_ref[...], bccur, cwbc_ref, cbbc_ref, n_prev)
    xprev_ref[...] = xcur[L - n_prev:]
    bcprev_ref[...] = bccur[L - n_prev:]

    dt = _softplus(dt_ref[...] + dtb_ref[...])
    a = dt * (-jnp.exp(alog_ref[...]))
    row = lax.broadcasted_iota(jnp.int32, (L, L), 0)
    col = lax.broadcasted_iota(jnp.int32, (L, L), 1)
    causal = row >= col
    acs = jnp.dot(causal.astype(f32), a, preferred_element_type=f32, precision=lax.Precision.HIGHEST)
    lane = lax.broadcasted_iota(jnp.int32, (L, LANES), 1)
    rows = jnp.where(lane < LANES // 2, acs, pltpu.roll(dt, LANES // 2, 1)).T
    acs_last = acs[L - 1:L, :]
    dec = jnp.exp(acs_last)
    eacs_x = _expand_heads(jnp.exp(acs), e_ref)
    seg_x = _expand_heads(jnp.exp(acs_last - acs) * dt, e_ref)
    xs = (xc * seg_x).astype(bf16)

    lo = lane < LANES // 2
    for g in range(G):
        sl = slice(g * cpg, (g + 1) * cpg)
        Bg = bcc[:, g * N:(g + 1) * N].astype(bf16)
        Cg = bcc[:, gw + g * N:gw + (g + 1) * N].astype(bf16)
        cb = lax.dot_general(Cg, Bg, _NT, preferred_element_type=f32)
        hg = st_ref[0, g]
        y = lax.dot_general(Cg, hg.astype(bf16), _NT, preferred_element_type=f32) * eacs_x[:, sl]
        y = y + xc[:, sl] * dx_ref[:, sl]
        ys = []
        for pp in range(hpg // 2):
            h0 = g * hpg + 2 * pp
            ms = []
            for e in (h0, h0 + 1):
                diff = acs[:, e:e + 1] - rows[e:e + 1, :]
                decay = jnp.exp(jnp.where(causal, diff, -jnp.inf))
                ms.append(cb * decay * rows[LANES // 2 + e:LANES // 2 + e + 1, :])
            xpair = xc[:, (h0 // 2) * LANES:(h0 // 2 + 1) * LANES]
            xbd = jnp.concatenate([jnp.where(lo, xpair, 0.0), jnp.where(lo, 0.0, xpair)], axis=0).astype(bf16)
            ys.append(jnp.dot(jnp.concatenate(ms, axis=1).astype(bf16), xbd, preferred_element_type=f32))
        y = y + jnp.concatenate(ys, axis=1)
        st = lax.dot_general(xs[:, sl], Bg, _TN, preferred_element_type=f32)
        decg = jnp.concatenate([jnp.broadcast_to(dec[:, h:h + 1], (P, N)) for h in range(g * hpg, (g + 1) * hpg)],
                               axis=0)
        st_ref[0, g] = decg * hg + st
        yg_ref[:, sl] = _gated_group_norm(y, z_ref[:, sl], ng_ref[:, sl]).astype(yg_ref.dtype)


def ssd_prompt(zx, dtp, cw, cb, dtb, alog, dx, ng, *, B, S, H, P, N, G):
    L = SSD_CHUNK
    nc = S // L
    di = H * P
    gw = 2 * G * N
    assert 2 * P == LANES and N == LANES and di % gw == 0 and H <= LANES // 2
    cwx, cwbc = cw[:, :di], cw[:, di:]
    cbx, cbbc = cb[:, :di], cb[:, di:]
    expand = (jnp.arange(LANES)[:, None] == jnp.arange(di)[None, :] // P).astype(bf16)
    kern = functools.partial(_ssd_prompt_kernel, L=L, H=H, N=N, G=G)
    const = lambda b, c: (0, 0)
    return pl.pallas_call(
        kern, grid=(B, nc),
        in_specs=[
            pl.BlockSpec((L, di), lambda b, c: (b * nc + c, 0)),
            pl.BlockSpec((L, di), lambda b, c: (b * nc + c, 1)),
            pl.BlockSpec((L, gw), lambda b, c: (b * nc + c, 2 * di // gw)),
            pl.BlockSpec((L, LANES), lambda b, c: (b * nc + c, 0)),
            pl.BlockSpec((CONV_W, di), const), pl.BlockSpec((CONV_W, gw), const),
            pl.BlockSpec((1, di), const), pl.BlockSpec((1, gw), const),
            pl.BlockSpec((1, LANES), const), pl.BlockSpec((1, LANES), const),
            pl.BlockSpec((1, di), const), pl.BlockSpec((1, di), const),
            pl.BlockSpec((LANES, di), const),
        ],
        out_specs=(pl.BlockSpec((L, di), lambda b, c: (b * nc + c, 0)),
                   pl.BlockSpec((1, G, di // G, N), lambda b, c: (b, 0, 0, 0))),
        out_shape=(jax.ShapeDtypeStruct((B * S, di), bf16), jax.ShapeDtypeStruct((B, G, di // G, N), f32)),
        scratch_shapes=[pltpu.VMEM((8, di), f32), pltpu.VMEM((8, gw), f32)],
        compiler_params=_params("parallel", "arbitrary"), name="ssd_prompt",
    )(zx, zx, zx, dtp, cwx, cwbc, cbx, cbbc, dtb, alog, dx, ng, expand)


def _ssd_sample_kernel(z_ref, x_ref, bc_ref, dtx_ref, cpx_ref, cpbc_ref, h0_ref, cwx_ref, cwbc_ref, cbx_ref,
                       cbbc_ref, dtb_ref, alog_ref, dx_ref, ng_ref, yg_ref, st_ref, *, L, N, G):
    di = x_ref.shape[1]
    gw = G * N
    cpg = di // G
    xc = _conv_silu(cpx_ref[0], x_ref[...], cwx_ref, cbx_ref, CONV_W - 1)
    bcc = _conv_silu(cpbc_ref[0], bc_ref[...], cwbc_ref, cbbc_ref, CONV_W - 1)
    dt = _softplus(dtx_ref[...] + dtb_ref[...])
    a = dt * (-jnp.exp(alog_ref[...]))
    row = lax.broadcasted_iota(jnp.int32, (L, di), 0)
    acs = jnp.zeros((L, di), f32)
    for s in range(L):
        acs = acs + jnp.where(row >= s, jnp.broadcast_to(a[s:s + 1, :], (L, di)), 0.0)
    acs_last = acs[L - 1:L, :]
    xdt = xc * dt

    cbx = []
    yoff = []
    for g in range(G):
        Bg = bcc[:, g * N:(g + 1) * N].astype(bf16)
        Cg = bcc[:, gw + g * N:gw + (g + 1) * N].astype(bf16)
        cb = lax.dot_general(Cg, Bg, _NT, preferred_element_type=f32)
        cbx.append([jnp.broadcast_to(cb[:, s:s + 1], (L, cpg)) for s in range(L)])
        hg = h0_ref[0, g]
        yoff.append(lax.dot_general(Cg, hg.astype(bf16), _NT, preferred_element_type=f32))
        sl = slice(g * cpg, (g + 1) * cpg)
        xd = (xdt[:, sl] * jnp.exp(acs_last[:, sl] - acs[:, sl])).astype(bf16)
        st = lax.dot_general(xd, Bg, _TN, preferred_element_type=f32)
        dcol = jnp.exp(jnp.broadcast_to(acs_last[:, sl], (L, cpg)).T[:, 0:1])
        st_ref[0, g] = dcol * hg + st
    y = jnp.exp(acs) * jnp.concatenate(yoff, axis=1) + xc * dx_ref[...]
    for s in range(L):
        decay = jnp.exp(jnp.where(row >= s, acs - jnp.broadcast_to(acs[s:s + 1, :], (L, di)), -jnp.inf))
        cbs = jnp.concatenate([cbx[g][s] for g in range(G)], axis=1)
        y = y + cbs * decay * jnp.broadcast_to(xdt[s:s + 1, :], (L, di))
    for g in range(G):
        sl = slice(g * cpg, (g + 1) * cpg)
        yg_ref[:, sl] = _gated_group_norm(y[:, sl], z_ref[:, sl], ng_ref[:, sl]).astype(yg_ref.dtype)


def ssd_sample(zx, dtx, conv_prev, h0, cw, cb, dtbx, alogx, dx, ng, *, B, L, row0, G, N):
    di = dtx.shape[1]
    gw = 2 * G * N
    cpg = di // G
    cwx, cwbc = cw[:, :di], cw[:, di:]
    cbx, cbbc = cb[:, :di], cb[:, di:]
    rb = row0 // L
    kern = functools.partial(_ssd_sample_kernel, L=L, N=N, G=G)
    const = lambda b: (0, 0)
    return pl.pallas_call(
        kern, grid=(B,),
        in_specs=[
            pl.BlockSpec((L, di), lambda b: (rb + b, 0)),
            pl.BlockSpec((L, di), lambda b: (rb + b, 1)),
            pl.BlockSpec((L, gw), lambda b: (rb + b, 2 * di // gw)),
            pl.BlockSpec((L, di), lambda b: (b, 0)),
            pl.BlockSpec((1, CONV_W - 1, di), lambda b: (b, 0, 0)),
            pl.BlockSpec((1, CONV_W - 1, gw), lambda b: (b, 0, di // gw)),
            pl.BlockSpec((1, G, cpg, N), lambda b: (b, 0, 0, 0)),
            pl.BlockSpec((CONV_W, di), const), pl.BlockSpec((CONV_W, gw), const),
            pl.BlockSpec((1, di), const), pl.BlockSpec((1, gw), const),
            pl.BlockSpec((1, di), const), pl.BlockSpec((1, di), const),
            pl.BlockSpec((1, di), const), pl.BlockSpec((1, di), const),
        ],
        out_specs=(pl.BlockSpec((L, di), lambda b: (b, 0)),
                   pl.BlockSpec((1, G, cpg, N), lambda b: (b, 0, 0, 0))),
        out_shape=(jax.ShapeDtypeStruct((B * L, di), f32), jax.ShapeDtypeStruct(h0.shape, f32)),
        compiler_params=_params("parallel"), name="ssd_sample",
    )(zx, zx, zx, dtx, conv_prev, conv_prev, h0, cwx, cwbc, cbx, cbbc, dtbx, alogx, dx, ng)


def _attn_prompt_kernel(qn_ref, qr_ref, kn_ref, v_ref, kr_ref, o_ref, kf_ref, *, tq, hb):
    qi = pl.program_id(2)
    dr, dk = kr_ref.shape[1], kf_ref.shape[2]
    dn, dv = kn_ref.shape[1] // hb, v_ref.shape[1] // hb

    @pl.when(qi == 0)
    def _():
        for j in range(hb):
            kf_ref[j, :, :dn] = kn_ref[:, j * dn:(j + 1) * dn]
            kf_ref[j, :, dn:dn + dr] = kr_ref[...]
            kf_ref[j, :, dn + dr:] = jnp.zeros((kf_ref.shape[1], dk - dn - dr), bf16)

    qs = [jnp.concatenate([qn_ref[:, j * dn:(j + 1) * dn], qr_ref[j], jnp.zeros((tq, dk - dn - dr), bf16)], axis=1)
          for j in range(hb)]

    def block(carry, k0, diagonal=False):
        ks = pl.ds(pl.multiple_of(k0, tq), tq)
        out = []
        for j, (m, l, acc) in enumerate(carry):
            s = lax.dot_general(qs[j], kf_ref[j, ks, :], _NT, preferred_element_type=f32)
            if diagonal:
                r = lax.broadcasted_iota(jnp.int32, (tq, tq), 0)
                c = lax.broadcasted_iota(jnp.int32, (tq, tq), 1)
                s = jnp.where(r >= c, s, -jnp.inf)
            m_new = jnp.maximum(m, jnp.max(s, axis=-1, keepdims=True))
            alpha = jnp.exp(m - m_new)
            p = jnp.exp(s - m_new)
            l = l * alpha + jnp.sum(p, axis=-1, keepdims=True)
            acc = acc * alpha + jnp.dot(p.astype(bf16), v_ref[ks, j * dv:(j + 1) * dv], preferred_element_type=f32)
            out.append((m_new, l, acc))
        return tuple(out)

    def pair(i, carry):
        return block(block(carry, 2 * i * tq), (2 * i + 1) * tq)

    carry = tuple((jnp.full((tq, 1), -jnp.inf, f32), jnp.zeros((tq, 1), f32), jnp.zeros((tq, dv), f32))
                  for _ in range(hb))
    carry = lax.fori_loop(0, qi // 2, pair, carry)
    carry = lax.cond(qi % 2 == 1, lambda c: block(c, (qi - 1) * tq), lambda c: c, carry)
    carry = block(carry, qi * tq, diagonal=True)
    for j, (_, l, acc) in enumerate(carry):
        o_ref[:, j * dv:(j + 1) * dv] = (acc / l).astype(o_ref.dtype)


def attn_prompt(qn, qr_hm, kv, krb, *, B, S, H, dn, dv, tq, hb):
    nq = S // tq
    dr = krb.shape[1]
    assert dn == dv and H % hb == 0
    return pl.pallas_call(
        functools.partial(_attn_prompt_kernel, tq=tq, hb=hb), grid=(B, H // hb, nq),
        in_specs=[
            pl.BlockSpec((tq, hb * dn), lambda b, h, i: (b * nq + i, h)),
            pl.BlockSpec((hb, tq, dr), lambda b, h, i: (h, b * nq + i, 0)),
            pl.BlockSpec((S, hb * dn), lambda b, h, i: (b, h)),
            pl.BlockSpec((S, hb * dv), lambda b, h, i: (b, H // hb + h)),
            pl.BlockSpec((S, dr), lambda b, h, i: (b, 0)),
        ],
        out_specs=pl.BlockSpec((tq, hb * dv), lambda b, h, i: (b * nq + i, h)),
        out_shape=jax.ShapeDtypeStruct((B * S, H * dv), bf16),
        scratch_shapes=[pltpu.VMEM((hb, S, 2 * LANES), bf16)],
        compiler_params=_params("parallel", "parallel", "arbitrary"), name="attn_prompt",
    )(qn, qr_hm, kv, kv, krb)


def _attn_sample_kernel(pt_ref, ql_ref, qr_ref, cn_ref, krn_ref, lat_hbm, krt_hbm, o_ref,
                        latbuf, krtbuf, lat_sem, krt_sem, cb_ref, kt_ref, *, n_seq, n_pages, G, P, H, sub):
    b = pl.program_id(0)
    ql = ql_ref[0]
    qr = qr_ref[0]
    R = ql.shape[0]
    pg = latbuf.shape[2]

    def group_copies(seq, g):
        slot = g % 2
        out = []
        for k in range(P):
            page = pt_ref[seq * n_pages + g * P + k]
            out.append(pltpu.make_async_copy(lat_hbm.at[page], latbuf.at[slot, k], lat_sem.at[slot]))
            out.append(pltpu.make_async_copy(krt_hbm.at[page], krtbuf.at[slot, k], krt_sem.at[slot]))
        return out

    @pl.when(b == 0)
    def _():
        for cp in group_copies(0, 0):
            cp.start()

    cn = cn_ref[0].astype(bf16)
    S = cn.shape[0]
    s = (lax.dot_general(ql, cn, _NT, preferred_element_type=f32)
         + lax.dot_general(qr, krn_ref[0].astype(bf16), _NT, preferred_element_type=f32))
    qpos = lax.broadcasted_iota(jnp.int32, (R, S), 0) // H
    kpos = lax.broadcasted_iota(jnp.int32, (R, S), 1)
    s = jnp.where(kpos <= qpos, s, -jnp.inf)
    m = jnp.max(s, axis=-1, keepdims=True)
    p = jnp.exp(s - m)
    l = jnp.sum(p, axis=-1, keepdims=True)
    acc = jnp.dot(p.astype(bf16), cn, preferred_element_type=f32)

    nsb = P // sub
    w = sub * pg
    for g in range(G):
        slot = g % 2
        if g + 1 < G:
            for cp in group_copies(b, g + 1):
                cp.start()
        else:
            @pl.when(b + 1 < n_seq)
            def _():
                for cp in group_copies(b + 1, 0):
                    cp.start()
        for cp in group_copies(b, g):
            cp.wait()
        ss = []
        for sb in range(nsb):
            for k in range(sb * sub, (sb + 1) * sub):
                cb_ref[k * pg:(k + 1) * pg, :] = latbuf[slot, k].astype(bf16)
                kt_ref[:, k * pg:(k + 1) * pg] = krtbuf[slot, k].astype(bf16)
            ss.append(lax.dot_general(ql, cb_ref[sb * w:(sb + 1) * w, :], _NT, preferred_element_type=f32)
                      + jnp.dot(qr, kt_ref[:, sb * w:(sb + 1) * w], preferred_element_type=f32))
        m_new = m
        for s in ss:
            m_new = jnp.maximum(m_new, jnp.max(s, axis=-1, keepdims=True))
        alpha = jnp.exp(m - m_new)
        m = m_new
        l = l * alpha
        acc = acc * alpha
        for sb, s in enumerate(ss):
            p = jnp.exp(s - m_new)
            l = l + jnp.sum(p, axis=-1, keepdims=True)
            acc = acc + jnp.dot(p.astype(bf16), cb_ref[sb * w:(sb + 1) * w, :], preferred_element_type=f32)
    o_ref[0] = (acc / l).astype(o_ref.dtype)


def attn_sample(page_table, ql, qr, c_new, kr_new, cache_lat, cache_krt, *, H, pages_per_group):
    B, R, C = ql.shape
    dr = qr.shape[2]
    S = c_new.shape[1]
    n_pages = page_table.shape[1]
    P = pages_per_group
    G = n_pages // P
    assert G * P == n_pages and G % 2 == 0, "page groups alternate between two buffer slots"
    pg = cache_lat.shape[1]
    seq = lambda b, pt_ref: (b, 0, 0)
    kern = functools.partial(_attn_sample_kernel, n_seq=B, n_pages=n_pages, G=G, P=P, H=H,
                             sub=min(P, SAMPLE_SUB_PAGES))
    return pl.pallas_call(
        kern,
        grid_spec=pltpu.PrefetchScalarGridSpec(
            num_scalar_prefetch=1, grid=(B,),
            in_specs=[pl.BlockSpec((1, R, C), seq), pl.BlockSpec((1, R, dr), seq),
                      pl.BlockSpec((1, S, C), seq), pl.BlockSpec((1, S, dr), seq),
                      pl.BlockSpec(memory_space=pl.ANY), pl.BlockSpec(memory_space=pl.ANY)],
            out_specs=pl.BlockSpec((1, R, C), seq),
            scratch_shapes=[pltpu.VMEM((2, P, pg, C), cache_lat.dtype), pltpu.VMEM((2, P, dr, pg), cache_krt.dtype),
                            pltpu.SemaphoreType.DMA((2,)), pltpu.SemaphoreType.DMA((2,)),
                            pltpu.VMEM((P * pg, C), bf16), pltpu.VMEM((dr, P * pg), bf16)]),
        out_shape=jax.ShapeDtypeStruct((B, R, C), bf16),
        compiler_params=_params("arbitrary"), name="attn_sample",
    )(page_table.reshape(-1), ql, qr, c_new, kr_new, cache_lat, cache_krt)


def _rope_tables(pos, half):
    inv = ROPE_THETA ** (-jnp.arange(half, dtype=f32) / half)
    ang = pos[:, None] * inv[None, :]
    cos, sin = jnp.cos(ang), jnp.sin(ang)
    return jnp.concatenate([cos, cos], axis=1), jnp.concatenate([-sin, sin], axis=1)


def _swap_halves(w, r):
    k, n = w.shape
    return w.reshape(k, n // r, 2, r // 2)[:, :, ::-1, :].reshape(k, n)


def kernel(x_prompt, x_sample, cache_kv_latent, cache_k_rope, state_ssm, state_conv, page_table, ffn_pre_g, ffn_post_g, ffn_w_gate, ffn_w_up, ffn_w_down, mix_pre_g, mix_post_g, m_in_proj, m_conv_w, m_conv_b, m_dt_bias, m_A_log, m_D, m_norm_g, m_out_proj, kv_norm_g, w_dkv, kv_latent_norm_g, w_uk, w_uv, q_w_dq, q_norm_g, q_w_uq, attn_w_o):
    Bp, S, D = x_prompt.shape
    Bs, Ls, _ = x_sample.shape
    Tp, Ts = Bp * S, Bs * Ls
    T = Tp + Ts
    depth = ffn_pre_g.shape[0]
    n_a = state_ssm.shape[0]
    assert depth == 2 and n_a == 1, "layer pattern: one SSD layer then one MLA layer"
    H, P, N = state_ssm.shape[2:]
    G = SSM_GROUPS
    di = H * P
    conv_dim = state_conv.shape[3]
    C = cache_kv_latent.shape[2]
    dr = cache_k_rope.shape[2]
    Hm, dn = w_uk.shape[1:]
    dv = w_uv.shape[2]
    past_len = page_table.shape[1] * cache_kv_latent.shape[1]
    scale = float(dn + dr) ** -0.5
    tm = 512
    row = lambda v: v.reshape(1, -1)

    wg_all, wu_all, wd_all = ffn_w_gate.astype(bf16), ffn_w_up.astype(bf16), ffn_w_down.astype(bf16)

    def ffn(h, layer, j, **kw):
        return ffn_half(h, row(ffn_pre_g[layer, j]), wg_all, wu_all, wd_all, row(ffn_post_g[layer, j]),
                        sel=(layer, j), tm=tm, tf=512, **kw)

    pos = jnp.concatenate([jnp.tile(jnp.arange(S, dtype=f32), Bp),
                           jnp.tile(past_len + jnp.arange(Ls, dtype=f32), Bs)])
    cos64, sin64 = _rope_tables(pos, dr // 2)
    cos128, sin128 = jnp.tile(cos64, (1, 2)), jnp.tile(sin64, (1, 2))

    h = ffn(x_prompt.reshape(Tp, D), 0, 0, hb=x_sample.reshape(Ts, D))
    w_in = m_in_proj[0]
    nzx = di + conv_dim
    w_zx = w_in[:, :nzx].astype(bf16)
    w_dt = w_in[:, nzx:]
    w_dtp = jnp.pad(w_dt, ((0, 0), (0, LANES - H))).astype(bf16)
    w_dtx = jnp.repeat(w_dt, P, axis=1).astype(bf16)
    g_mix0 = row(mix_pre_g[0])
    zx = fused_matmul(h, w_zx, tm=2 * tm, tn=1024, g_pre=g_mix0, name="in_proj_zx")
    dtp = fused_matmul(h, w_dtp, tm=tm, g_pre=g_mix0, n_rows=Tp, name="in_proj_dt")
    dtx = fused_matmul(h, w_dtx, tm=tm, tn=1024, g_pre=g_mix0, n_rows=Ts, row_block_offset=Tp // tm,
                       name="in_proj_dtx")
    cw, cb = m_conv_w[0], row(m_conv_b[0])
    dx = row(jnp.repeat(m_D[0], P))
    ng = row(m_norm_g[0])
    pad_h = lambda v: row(jnp.pad(v, (0, LANES - H)))
    yg_p, ssm_p = ssd_prompt(zx, dtp, cw, cb, pad_h(m_dt_bias[0]), pad_h(m_A_log[0]), dx, ng,
                             B=Bp, S=S, H=H, P=P, N=N, G=G)
    yg_s, ssm_s = ssd_sample(zx, dtx, state_conv[0], state_ssm[0].reshape(Bs, G, di // G, N), cw, cb,
                             row(jnp.repeat(m_dt_bias[0], P)), row(jnp.repeat(m_A_log[0], P)), dx, ng,
                             B=Bs, L=Ls, row0=Tp, G=G, N=N)
    h = fused_matmul(yg_p, m_out_proj[0].astype(bf16), xb=yg_s.astype(bf16), tm=tm, tn=512, mode="resnorm", res=h,
                     g_post=row(mix_post_g[0]), name="out_proj")
    h = ffn(h, 0, 1)

    tail = CONV_W - 1
    conv_prompt = jnp.stack([lax.slice(zx, ((b + 1) * S - tail, di), ((b + 1) * S, nzx)) for b in range(Bp)])[None]
    conv_sample = zx[Tp:].reshape(Bs, Ls, -1)[:, Ls - tail:, di:][None]
    ssm_prompt = ssm_p.reshape(1, Bp, H, P, N)
    ssm_sample = ssm_s.reshape(1, Bs, H, P, N)

    w_kv = jnp.concatenate([w_dkv, _swap_halves(w_dkv[:, C:], dr)], axis=1).astype(bf16)
    lat, kr, krb = fused_matmul(h, w_kv, tm=tm, mode="kv", g_pre=row(kv_norm_g), g_post=row(kv_latent_norm_g),
                                cos=cos64, sin=sin64, n_lat=C, name="shared_kv")

    h = ffn(h, 1, 0)
    cq = fused_matmul(h, q_w_dq[0].astype(bf16), tm=tm, mode="norm", g_pre=row(mix_pre_g[1]),
                      g_post=row(q_norm_g[0]), out_dtype=bf16, name="q_down")
    w_uq = q_w_uq[0]
    qrank = w_uq.shape[0]
    w_qn = w_uq[:, :, :dn].reshape(qrank, Hm * dn).astype(bf16)
    w_qr = w_uq[:, :, dn:].reshape(qrank, Hm * dr)
    w_qr2 = jnp.concatenate([w_qr, _swap_halves(w_qr, dr)], axis=1).astype(bf16)
    qn = fused_matmul(cq, w_qn, tm=tm, scale=scale, out_dtype=bf16, name="q_nope")
    qr = fused_matmul(cq, w_qr2, tm=tm, mode="rope", cos=cos128, sin=sin128, scale=scale, out_dtype=bf16,
                      name="q_rope")

    w_kvup = jnp.concatenate([w_uk.reshape(C, Hm * dn), w_uv.reshape(C, Hm * dv)], axis=1).astype(bf16)
    kv = fused_matmul(lat, w_kvup, tm=tm, tn=1024, n_rows=Tp, out_dtype=bf16, name="kv_up")
    qr_hm = qr.reshape(T, Hm, dr).transpose(1, 0, 2)
    o_p = attn_prompt(qn, qr_hm, kv, krb, B=Bp, S=S, H=Hm, dn=dn, dv=dv, tq=512, hb=2)

    w_ukh = w_uk.transpose(1, 2, 0).astype(bf16)
    w_uvh = w_uv.transpose(1, 0, 2).astype(bf16)
    ql = head_matmul(qn[Tp:], w_ukh, name="q_absorb")
    o_lat = attn_sample(page_table, ql.reshape(Bs, Ls * Hm, C), qr[Tp:].reshape(Bs, Ls * Hm, dr),
                        lat[Tp:].reshape(Bs, Ls, C), kr[Tp:].reshape(Bs, Ls, dr),
                        cache_kv_latent, cache_k_rope.transpose(0, 2, 1), H=Hm,
                        pages_per_group=min(32, page_table.shape[1] // 2))
    o_s = head_matmul(o_lat.reshape(Ts, Hm * C), w_uvh, name="v_up")

    h = fused_matmul(o_p, attn_w_o[0].reshape(Hm * dv, D).astype(bf16), xb=o_s, tm=tm, tn=512, mode="resnorm",
                     res=h, g_post=row(mix_post_g[1]), name="attn_out")
    y_p, y_s = ffn(h, 1, 1, split_out=Tp)

    y_prompt = y_p.reshape(Bp, S, D)
    y_sample = y_s.reshape(Bs, Ls, D)
    return (y_prompt, y_sample, lat[:Tp].reshape(Bp, S, C), kr[:Tp].reshape(Bp, S, dr), ssm_prompt, conv_prompt,
            lat[Tp:].reshape(Bs, Ls, C), kr[Tp:].reshape(Bs, Ls, dr), ssm_sample, conv_sample)
```

```python
import functools

import jax
import jax.numpy as jnp
from jax import lax
from jax.experimental import pallas as pl
from jax.experimental.pallas import tpu as pltpu

f32 = jnp.float32
bf16 = jnp.bfloat16

RMS_EPS = 1e-6
ROPE_THETA = 10000.0
SSD_CHUNK = 128
SSM_GROUPS = 8
CONV_W = 4
SAMPLE_SUB_PAGES = 8
LANES = 128
V7X_VMEM_BYTES = 64 * 1024 * 1024
VMEM_LIMIT = V7X_VMEM_BYTES - 8 * 1024 * 1024

_NT = (((1,), (1,)), ((), ()))
_TN = (((0,), (0,)), ((), ()))


def _rms(x, g):
    return x * lax.rsqrt(jnp.mean(x * x, axis=-1, keepdims=True) + RMS_EPS) * g


def _silu(x):
    return x * jax.nn.sigmoid(x)


def _softplus(x):
    return jnp.maximum(x, 0.0) + jnp.log1p(jnp.exp(-jnp.abs(x)))


def _params(*sem):
    return pltpu.CompilerParams(dimension_semantics=sem, vmem_limit_bytes=VMEM_LIMIT)


def _ffn_kernel(*refs, nj, dchunk, na, dual_in, dual_out):
    refs = list(refs)
    h_refs = [refs.pop(0) for _ in range(2 if dual_in else 1)]
    gpre_ref, wg_ref, wu_ref, wd_ref, gpost_ref = refs[:5]
    o_refs = refs[5:7] if dual_out else refs[5:6]
    xn_ref, acc_ref = refs[-2:] if dual_out else (refs[-1], o_refs[0])
    i = pl.program_id(0)
    j = pl.program_id(1)
    sides = [(i < na, h_refs[0], o_refs[0]), (i >= na, h_refs[-1], o_refs[-1])] if dual_in or dual_out \
        else [(True, h_refs[0], o_refs[0])]

    for cond, h_ref, _ in sides:
        @pl.when((j == 0) & cond)
        def _(h_ref=h_ref):
            xn_ref[...] = _rms(h_ref[...], gpre_ref[...]).astype(bf16)
            acc_ref[...] = jnp.zeros_like(acc_ref)

    xn = xn_ref[...]
    g = jnp.dot(xn, wg_ref[...], preferred_element_type=f32)
    u = jnp.dot(xn, wu_ref[...], preferred_element_type=f32)
    a = (_silu(g) * u).astype(bf16)
    for c in range(0, acc_ref.shape[1], dchunk):
        acc_ref[:, c:c + dchunk] += jnp.dot(a, wd_ref[:, c:c + dchunk], preferred_element_type=f32)

    for cond, h_ref, o_ref in sides:
        @pl.when((j == nj - 1) & cond)
        def _(h_ref=h_ref, o_ref=o_ref):
            o_ref[...] = h_ref[...] + 0.5 * _rms(acc_ref[...], gpost_ref[...])


def ffn_half(h, g_pre, wg, wu, wd, g_post, *, sel, tm, tf, hb=None, split_out=None):
    D = h.shape[1]
    F = wg.shape[3]
    nj = F // tf
    T = h.shape[0] + (hb.shape[0] if hb is not None else 0)
    dual_in, dual_out = hb is not None, split_out is not None
    n_first = h.shape[0] if dual_in else (split_out if dual_out else T)
    na = n_first // tm
    assert na * tm == n_first and T % tm == 0
    first = lambda i, j: (jnp.minimum(i, na - 1), 0)
    second = lambda i, j: (jnp.maximum(i - na, 0), 0)
    whole = lambda i, j: (i, 0)
    const = lambda i, j: (0, 0)
    h_args = [h, hb] if dual_in else [h]
    h_specs = [pl.BlockSpec((tm, D), first), pl.BlockSpec((tm, D), second)] if dual_in else [pl.BlockSpec((tm, D), whole)]
    if dual_out:
        out_specs = (pl.BlockSpec((tm, D), first), pl.BlockSpec((tm, D), second))
        out_shape = (jax.ShapeDtypeStruct((n_first, D), f32), jax.ShapeDtypeStruct((T - n_first, D), f32))
    else:
        out_specs = pl.BlockSpec((tm, D), whole)
        out_shape = jax.ShapeDtypeStruct((T, D), f32)
    return pl.pallas_call(
        functools.partial(_ffn_kernel, nj=nj, dchunk=min(D, 512), na=na, dual_in=dual_in, dual_out=dual_out),
        grid=(T // tm, nj),
        in_specs=h_specs + [
            pl.BlockSpec((1, D), const),
            pl.BlockSpec((None, None, D, tf), lambda i, j: (*sel, 0, j)),
            pl.BlockSpec((None, None, D, tf), lambda i, j: (*sel, 0, j)),
            pl.BlockSpec((None, None, tf, D), lambda i, j: (*sel, j, 0)),
            pl.BlockSpec((1, D), const),
        ],
        out_specs=out_specs,
        out_shape=out_shape,
        scratch_shapes=[pltpu.VMEM((tm, D), bf16)] + ([pltpu.VMEM((tm, D), f32)] if dual_out else []),
        compiler_params=_params("arbitrary" if dual_out else "parallel", "arbitrary"),
        name="ffn_half",
    )(*h_args, g_pre, wg, wu, wd, g_post)


def _mm_kernel(*refs, prenorm, dual, na, mode, nj, tn, scale, use_xn, n_lat):
    refs = list(refs)
    xa_ref = refs.pop(0)
    xb_ref = refs.pop(0) if dual else None
    gpre_ref = refs.pop(0) if prenorm else None
    w_ref = refs.pop(0)
    i = pl.program_id(0)
    j = pl.program_id(1)

    def load_x(x_ref):
        x = x_ref[...]
        if prenorm:
            x = _rms(x.astype(f32), gpre_ref[...])
        return x.astype(bf16)

    if use_xn:
        xn_ref = refs[-1] if mode in ("plain", "kv", "rope") or nj == 1 else refs[-2]

        if dual:
            @pl.when((j == 0) & (i < na))
            def _():
                xn_ref[...] = load_x(xa_ref)

            @pl.when((j == 0) & (i >= na))
            def _():
                xn_ref[...] = load_x(xb_ref)
        else:
            @pl.when(j == 0)
            def _():
                xn_ref[...] = load_x(xa_ref)

        xn = xn_ref[...]
    else:
        xn = xa_ref[...]
    acc = jnp.dot(xn, w_ref[...], preferred_element_type=f32)

    if mode == "plain":
        o_ref = refs[0]
        o_ref[...] = (acc * scale if scale != 1.0 else acc).astype(o_ref.dtype)
    elif mode in ("norm", "resnorm"):
        if mode == "resnorm":
            res_ref, gpost_ref, o_ref = refs[0], refs[1], refs[2]
        else:
            res_ref, gpost_ref, o_ref = None, refs[0], refs[1]

        def finish(slabs):
            n = nj * tn
            ssq = sum(jnp.sum(s * s, axis=-1, keepdims=True) for s in slabs)
            inv = lax.rsqrt(ssq * (1.0 / n) + RMS_EPS)
            for k, s in enumerate(slabs):
                y = s * inv * gpost_ref[:, k * tn:(k + 1) * tn]
                if res_ref is not None:
                    y = res_ref[:, k * tn:(k + 1) * tn] + y
                o_ref[:, k * tn:(k + 1) * tn] = y.astype(o_ref.dtype)

        if nj == 1:
            finish([acc])
        else:
            slab_ref = refs[-1]
            slab_ref[j] = acc

            @pl.when(j == nj - 1)
            def _():
                finish([slab_ref[k] for k in range(nj)])
    elif mode == "kv":
        gpost_ref, cos_ref, sin_ref, lat_ref, kr_ref, krb_ref = refs[:6]
        lat_ref[...] = _rms(acc[:, :n_lat], gpost_ref[...])
        r = cos_ref.shape[1]
        kr = acc[:, n_lat:n_lat + r] * cos_ref[...] + acc[:, n_lat + r:n_lat + 2 * r] * sin_ref[...]
        kr_ref[...] = kr
        krb_ref[...] = kr.astype(bf16)
    elif mode == "rope":
        cos_ref, sin_ref, o_ref = refs[:3]
        half = acc.shape[1] // 2
        reps = half // cos_ref.shape[1]
        cos = jnp.concatenate([cos_ref[...]] * reps, axis=1)
        sin = jnp.concatenate([sin_ref[...]] * reps, axis=1)
        o_ref[...] = ((acc[:, :half] * cos + acc[:, half:] * sin) * scale).astype(o_ref.dtype)
    else:
        raise ValueError(mode)


def fused_matmul(x, w, *, tm, tn=None, mode="plain", g_pre=None, xb=None, res=None, g_post=None,
                 cos=None, sin=None, scale=1.0, out_dtype=f32, n_rows=None, row_block_offset=0,
                 n_lat=0, name="fused_matmul"):
    K, N = w.shape
    if tn is None:
        tn = N
    nj = N // tn
    dual = xb is not None
    na = x.shape[0] // tm
    M = (x.shape[0] + (xb.shape[0] if dual else 0)) if n_rows is None else n_rows
    ni = M // tm
    off = row_block_offset
    prenorm = g_pre is not None
    use_xn = prenorm or dual or x.dtype != bf16

    args, in_specs = [], []
    if dual:
        args += [x, xb]
        in_specs += [pl.BlockSpec((tm, K), lambda i, j: (jnp.minimum(i, na - 1), 0)),
                     pl.BlockSpec((tm, K), lambda i, j: (jnp.maximum(i - na, 0), 0))]
    else:
        args.append(x)
        in_specs.append(pl.BlockSpec((tm, K), lambda i, j: (i + off, 0)))
    if prenorm:
        args.append(g_pre)
        in_specs.append(pl.BlockSpec((1, K), lambda i, j: (0, 0)))
    args.append(w)
    in_specs.append(pl.BlockSpec((K, tn), lambda i, j: (0, j)))

    scratch = []
    if mode == "plain":
        out_shape = jax.ShapeDtypeStruct((M, N), out_dtype)
        out_specs = pl.BlockSpec((tm, tn), lambda i, j: (i, j))
    elif mode in ("norm", "resnorm"):
        if mode == "resnorm":
            args.append(res)
            in_specs.append(pl.BlockSpec((tm, N), lambda i, j: (i + off, 0)))
        args.append(g_post)
        in_specs.append(pl.BlockSpec((1, N), lambda i, j: (0, 0)))
        out_shape = jax.ShapeDtypeStruct((M, N), out_dtype)
        out_specs = pl.BlockSpec((tm, N), lambda i, j: (i, 0))
    elif mode == "kv":
        assert nj == 1
        r = cos.shape[1]
        args += [g_post, cos, sin]
        in_specs += [pl.BlockSpec((1, n_lat), lambda i, j: (0, 0)),
                     pl.BlockSpec((tm, r), lambda i, j: (i + off, 0)),
                     pl.BlockSpec((tm, r), lambda i, j: (i + off, 0))]
        out_shape = (jax.ShapeDtypeStruct((M, n_lat), f32), jax.ShapeDtypeStruct((M, r), f32),
                     jax.ShapeDtypeStruct((M, r), bf16))
        out_specs = (pl.BlockSpec((tm, n_lat), lambda i, j: (i, 0)), pl.BlockSpec((tm, r), lambda i, j: (i, 0)),
                     pl.BlockSpec((tm, r), lambda i, j: (i, 0)))
    elif mode == "rope":
        assert nj == 1
        r = cos.shape[1]
        args += [cos, sin]
        in_specs += [pl.BlockSpec((tm, r), lambda i, j: (i + off, 0)),
                     pl.BlockSpec((tm, r), lambda i, j: (i + off, 0))]
        out_shape = jax.ShapeDtypeStruct((M, N // 2), out_dtype)
        out_specs = pl.BlockSpec((tm, N // 2), lambda i, j: (i, 0))
    else:
        raise ValueError(mode)
    if mode in ("norm", "resnorm") and nj > 1:
        if use_xn:
            scratch.append(pltpu.VMEM((tm, K), bf16))
        scratch.append(pltpu.VMEM((nj, tm, tn), f32))
    elif use_xn:
        scratch.append(pltpu.VMEM((tm, K), bf16))

    kern = functools.partial(_mm_kernel, prenorm=prenorm, dual=dual, na=na, mode=mode, nj=nj, tn=tn,
                             scale=scale, use_xn=use_xn, n_lat=n_lat)
    return pl.pallas_call(
        kern, grid=(ni, nj), in_specs=in_specs, out_specs=out_specs, out_shape=out_shape,
        scratch_shapes=scratch, compiler_params=_params("parallel", "arbitrary"), name=name,
    )(*args)


def _bmm_kernel(x_ref, w_ref, o_ref):
    o_ref[...] = jnp.dot(x_ref[...], w_ref[0], preferred_element_type=f32).astype(o_ref.dtype)


def head_matmul(x, w, *, out_dtype=bf16, name="head_matmul"):
    H, K, N = w.shape
    M = x.shape[0]
    return pl.pallas_call(
        _bmm_kernel, grid=(H,),
        in_specs=[pl.BlockSpec((M, K), lambda h: (0, h)), pl.BlockSpec((1, K, N), lambda h: (h, 0, 0))],
        out_specs=pl.BlockSpec((M, N), lambda h: (0, h)),
        out_shape=jax.ShapeDtypeStruct((M, H * N), out_dtype),
        compiler_params=_params("parallel"), name=name,
    )(x, w)


def _conv_silu(prev, cur, w_ref, b_ref, n_prev):
    L = cur.shape[0]
    xp = jnp.concatenate([prev, cur], axis=0)
    base = n_prev - (CONV_W - 1)
    acc = b_ref[...] + xp[base:base + L] * w_ref[0:1, :]
    for k in range(1, CONV_W):
        acc = acc + xp[base + k:base + k + L] * w_ref[k:k + 1, :]
    return _silu(acc)


def _gated_group_norm(y, z, g):
    v = y * _silu(z)
    return v * lax.rsqrt(jnp.mean(v * v, axis=-1, keepdims=True) + RMS_EPS) * g


def _expand_heads(v, e_ref):
    hi = v.astype(bf16)
    lo = (v - hi.astype(f32)).astype(bf16)
    e = e_ref[...]
    return jnp.dot(hi, e, preferred_element_type=f32) + jnp.dot(lo, e, preferred_element_type=f32)


def _ssd_prompt_kernel(z_ref, x_ref, bc_ref, dt_ref, cwx_ref, cwbc_ref, cbx_ref, cbbc_ref, dtb_ref, alog_ref,
                       dx_ref, ng_ref, e_ref, yg_ref, st_ref, xprev_ref, bcprev_ref, *, L, H, N, G):
    c = pl.program_id(1)
    n_prev = xprev_ref.shape[0]
    gw = G * N
    hpg = H // G
    cpg = x_ref.shape[1] // G
    P = cpg // hpg

    @pl.when(c == 0)
    def _():
        st_ref[...] = jnp.zeros_like(st_ref)
        xprev_ref[...] = jnp.zeros_like(xprev_ref)
        bcprev_ref[...] = jnp.zeros_like(bcprev_ref)

    xcur = x_ref[...]
    bccur = bc_ref[...]
    xc = _conv_silu(xprev_ref[...], xcur, cwx_ref, cbx_ref, n_prev)
    bcc = _conv_silu(bcprev_ref[...], bccur, cwbc_ref, cbbc_ref, n_prev)
    xprev_ref[...] = xcur[L - n_prev:]
    bcprev_ref[...] = bccur[L - n_prev:]

    dt = _softplus(dt_ref[...] + dtb_ref[...])
    a = dt * (-jnp.exp(alog_ref[...]))
    row = lax.broadcasted_iota(jnp.int32, (L, L), 0)
    col = lax.broadcasted_iota(jnp.int32, (L, L), 1)
    causal = row >= col
    acs = jnp.dot(causal.astype(f32), a, preferred_element_type=f32, precision=lax.Precision.HIGHEST)
    lane = lax.broadcasted_iota(jnp.int32, (L, LANES), 1)
    rows = jnp.where(lane < LANES // 2, acs, pltpu.roll(dt, LANES // 2, 1)).T
    acs_last = acs[L - 1:L, :]
    dec = jnp.exp(acs_last)
    eacs_x = _expand_heads(jnp.exp(acs), e_ref)
    seg_x = _expand_heads(jnp.exp(acs_last - acs) * dt, e_ref)
    xs = (xc * seg_x).astype(bf16)

    lo = lane < LANES // 2
    for g in range(G):
        sl = slice(g * cpg, (g + 1) * cpg)
        Bg = bcc[:, g * N:(g + 1) * N].astype(bf16)
        Cg = bcc[:, gw + g * N:gw + (g + 1) * N].astype(bf16)
        cb = lax.dot_general(Cg, Bg, _NT, preferred_element_type=f32)
        hg = st_ref[0, g]
        y = lax.dot_general(Cg, hg.astype(bf16), _NT, preferred_element_type=f32) * eacs_x[:, sl]
        y = y + xc[:, sl] * dx_ref[:, sl]
        ys = []
        for pp in range(hpg // 2):
            h0 = g * hpg + 2 * pp
            ms = []
            for e in (h0, h0 + 1):
                diff = acs[:, e:e + 1] - rows[e:e + 1, :]
                decay = jnp.exp(jnp.where(causal, diff, -jnp.inf))
                ms.append(cb * decay * rows[LANES // 2 + e:LANES // 2 + e + 1, :])
            xpair = xc[:, (h0 // 2) * LANES:(h0 // 2 + 1) * LANES]
            xbd = jnp.concatenate([jnp.where(lo, xpair, 0.0), jnp.where(lo, 0.0, xpair)], axis=0).astype(bf16)
            ys.append(jnp.dot(jnp.concatenate(ms, axis=1).astype(bf16), xbd, preferred_element_type=f32))
        y = y + jnp.concatenate(ys, axis=1)
        st = lax.dot_general(xs[:, sl], Bg, _TN, preferred_element_type=f32)
        decg = jnp.concatenate([jnp.broadcast_to(dec[:, h:h + 1], (P, N)) for h in range(g * hpg, (g + 1) * hpg)],
                               axis=0)
        st_ref[0, g] = decg * hg + st
        yg_ref[:, sl] = _gated_group_norm(y, z_ref[:, sl], ng_ref[:, sl]).astype(yg_ref.dtype)


def ssd_prompt(zx, dtp, cw, cb, dtb, alog, dx, ng, *, B, S, H, P, N, G):
    L = SSD_CHUNK
    nc = S // L
    di = H * P
    gw = 2 * G * N
    assert 2 * P == LANES and N == LANES and di % gw == 0 and H <= LANES // 2
    cwx, cwbc = cw[:, :di], cw[:, di:]
    cbx, cbbc = cb[:, :di], cb[:, di:]
    expand = (jnp.arange(LANES)[:, None] == jnp.arange(di)[None, :] // P).astype(bf16)
    kern = functools.partial(_ssd_prompt_kernel, L=L, H=H, N=N, G=G)
    const = lambda b, c: (0, 0)
    return pl.pallas_call(
        kern, grid=(B, nc),
        in_specs=[
            pl.BlockSpec((L, di), lambda b, c: (b * nc + c, 0)),
            pl.BlockSpec((L, di), lambda b, c: (b * nc + c, 1)),
            pl.BlockSpec((L, gw), lambda b, c: (b * nc + c, 2 * di // gw)),
            pl.BlockSpec((L, LANES), lambda b, c: (b * nc + c, 0)),
            pl.BlockSpec((CONV_W, di), const), pl.BlockSpec((CONV_W, gw), const),
            pl.BlockSpec((1, di), const), pl.BlockSpec((1, gw), const),
            pl.BlockSpec((1, LANES), const), pl.BlockSpec((1, LANES), const),
            pl.BlockSpec((1, di), const), pl.BlockSpec((1, di), const),
            pl.BlockSpec((LANES, di), const),
        ],
        out_specs=(pl.BlockSpec((L, di), lambda b, c: (b * nc + c, 0)),
                   pl.BlockSpec((1, G, di // G, N), lambda b, c: (b, 0, 0, 0))),
        out_shape=(jax.ShapeDtypeStruct((B * S, di), bf16), jax.ShapeDtypeStruct((B, G, di // G, N), f32)),
        scratch_shapes=[pltpu.VMEM((8, di), f32), pltpu.VMEM((8, gw), f32)],
        compiler_params=_params("parallel", "arbitrary"), name="ssd_prompt",
    )(zx, zx, zx, dtp, cwx, cwbc, cbx, cbbc, dtb, alog, dx, ng, expand)


def _ssd_sample_kernel(z_ref, x_ref, bc_ref, dtx_ref, cpx_ref, cpbc_ref, h0_ref, cwx_ref, cwbc_ref, cbx_ref,
                       cbbc_ref, dtb_ref, alog_ref, dx_ref, ng_ref, yg_ref, st_ref, *, L, N, G):
    di = x_ref.shape[1]
    gw = G * N
    cpg = di // G
    xc = _conv_silu(cpx_ref[0], x_ref[...], cwx_ref, cbx_ref, CONV_W - 1)
    bcc = _conv_silu(cpbc_ref[0], bc_ref[...], cwbc_ref, cbbc_ref, CONV_W - 1)
    dt = _softplus(dtx_ref[...] + dtb_ref[...])
    a = dt * (-jnp.exp(alog_ref[...]))
    row = lax.broadcasted_iota(jnp.int32, (L, di), 0)
    acs = jnp.zeros((L, di), f32)
    for s in range(L):
        acs = acs + jnp.where(row >= s, jnp.broadcast_to(a[s:s + 1, :], (L, di)), 0.0)
    acs_last = acs[L - 1:L, :]
    xdt = xc * dt

    cbx = []
    yoff = []
    for g in range(G):
        Bg = bcc[:, g * N:(g + 1) * N].astype(bf16)
        Cg = bcc[:, gw + g * N:gw + (g + 1) * N].astype(bf16)
        cb = lax.dot_general(Cg, Bg, _NT, preferred_element_type=f32)
        cbx.append([jnp.broadcast_to(cb[:, s:s + 1], (L, cpg)) for s in range(L)])
        hg = h0_ref[0, g]
        yoff.append(lax.dot_general(Cg, hg.astype(bf16), _NT, preferred_element_type=f32))
        sl = slice(g * cpg, (g + 1) * cpg)
        xd = (xdt[:, sl] * jnp.exp(acs_last[:, sl] - acs[:, sl])).astype(bf16)
        st = lax.dot_general(xd, Bg, _TN, preferred_element_type=f32)
        dcol = jnp.exp(jnp.broadcast_to(acs_last[:, sl], (L, cpg)).T[:, 0:1])
        st_ref[0, g] = dcol * hg + st
    y = jnp.exp(acs) * jnp.concatenate(yoff, axis=1) + xc * dx_ref[...]
    for s in range(L):
        decay = jnp.exp(jnp.where(row >= s, acs - jnp.broadcast_to(acs[s:s + 1, :], (L, di)), -jnp.inf))
        cbs = jnp.concatenate([cbx[g][s] for g in range(G)], axis=1)
        y = y + cbs * decay * jnp.broadcast_to(xdt[s:s + 1, :], (L, di))
    for g in range(G):
        sl = slice(g * cpg, (g + 1) * cpg)
        yg_ref[:, sl] = _gated_group_norm(y[:, sl], z_ref[:, sl], ng_ref[:, sl]).astype(yg_ref.dtype)


def ssd_sample(zx, dtx, conv_prev, h0, cw, cb, dtbx, alogx, dx, ng, *, B, L, row0, G, N):
    di = dtx.shape[1]
    gw = 2 * G * N
    cpg = di // G
    cwx, cwbc = cw[:, :di], cw[:, di:]
    cbx, cbbc = cb[:, :di], cb[:, di:]
    rb = row0 // L
    kern = functools.partial(_ssd_sample_kernel, L=L, N=N, G=G)
    const = lambda b: (0, 0)
    return pl.pallas_call(
        kern, grid=(B,),
        in_specs=[
            pl.BlockSpec((L, di), lambda b: (rb + b, 0)),
            pl.BlockSpec((L, di), lambda b: (rb + b, 1)),
            pl.BlockSpec((L, gw), lambda b: (rb + b, 2 * di // gw)),
            pl.BlockSpec((L, di), lambda b: (b, 0)),
            pl.BlockSpec((1, CONV_W - 1, di), lambda b: (b, 0, 0)),
            pl.BlockSpec((1, CONV_W - 1, gw), lambda b: (b, 0, di // gw)),
            pl.BlockSpec((1, G, cpg, N), lambda b: (b, 0, 0, 0)),
            pl.BlockSpec((CONV_W, di), const), pl.BlockSpec((CONV_W, gw), const),
            pl.BlockSpec((1, di), const), pl.BlockSpec((1, gw), const),
            pl.BlockSpec((1, di), const), pl.BlockSpec((1, di), const),
            pl.BlockSpec((1, di), const), pl.BlockSpec((1, di), const),
        ],
        out_specs=(pl.BlockSpec((L, di), lambda b: (b, 0)),
                   pl.BlockSpec((1, G, cpg, N), lambda b: (b, 0, 0, 0))),
        out_shape=(jax.ShapeDtypeStruct((B * L, di), f32), jax.ShapeDtypeStruct(h0.shape, f32)),
        compiler_params=_params("parallel"), name="ssd_sample",
    )(zx, zx, zx, dtx, conv_prev, conv_prev, h0, cwx, cwbc, cbx, cbbc, dtbx, alogx, dx, ng)


def _attn_prompt_kernel(qn_ref, qr_ref, kn_ref, v_ref, kr_ref, o_ref, kf_ref, *, tq, hb):
    qi = pl.program_id(2)
    dr, dk = kr_ref.shape[1], kf_ref.shape[2]
    dn, dv = kn_ref.shape[1] // hb, v_ref.shape[1] // hb

    @pl.when(qi == 0)
    def _():
        for j in range(hb):
            kf_ref[j, :, :dn] = kn_ref[:, j * dn:(j + 1) * dn]
            kf_ref[j, :, dn:dn + dr] = kr_ref[...]
            kf_ref[j, :, dn + dr:] = jnp.zeros((kf_ref.shape[1], dk - dn - dr), bf16)

    qs = [jnp.concatenate([qn_ref[:, j * dn:(j + 1) * dn], qr_ref[j], jnp.zeros((tq, dk - dn - dr), bf16)], axis=1)
          for j in range(hb)]

    def block(carry, k0, diagonal=False):
        ks = pl.ds(pl.multiple_of(k0, tq), tq)
        out = []
        for j, (m, l, acc) in enumerate(carry):
            s = lax.dot_general(qs[j], kf_ref[j, ks, :], _NT, preferred_element_type=f32)
            if diagonal:
                r = lax.broadcasted_iota(jnp.int32, (tq, tq), 0)
                c = lax.broadcasted_iota(jnp.int32, (tq, tq), 1)
                s = jnp.where(r >= c, s, -jnp.inf)
            m_new = jnp.maximum(m, jnp.max(s, axis=-1, keepdims=True))
            alpha = jnp.exp(m - m_new)
            p = jnp.exp(s - m_new)
            l = l * alpha + jnp.sum(p, axis=-1, keepdims=True)
            acc = acc * alpha + jnp.dot(p.astype(bf16), v_ref[ks, j * dv:(j + 1) * dv], preferred_element_type=f32)
            out.append((m_new, l, acc))
        return tuple(out)

    def pair(i, carry):
        return block(block(carry, 2 * i * tq), (2 * i + 1) * tq)

    carry = tuple((jnp.full((tq, 1), -jnp.inf, f32), jnp.zeros((tq, 1), f32), jnp.zeros((tq, dv), f32))
                  for _ in range(hb))
    carry = lax.fori_loop(0, qi // 2, pair, carry)
    carry = lax.cond(qi % 2 == 1, lambda c: block(c, (qi - 1) * tq), lambda c: c, carry)
    carry = block(carry, qi * tq, diagonal=True)
    for j, (_, l, acc) in enumerate(carry):
        o_ref[:, j * dv:(j + 1) * dv] = (acc / l).astype(o_ref.dtype)


def attn_prompt(qn, qr_hm, kv, krb, *, B, S, H, dn, dv, tq, hb):
    nq = S // tq
    dr = krb.shape[1]
    assert dn == dv and H % hb == 0
    return pl.pallas_call(
        functools.partial(_attn_prompt_kernel, tq=tq, hb=hb), grid=(B, H // hb, nq),
        in_specs=[
            pl.BlockSpec((tq, hb * dn), lambda b, h, i: (b * nq + i, h)),
            pl.BlockSpec((hb, tq, dr), lambda b, h, i: (h, b * nq + i, 0)),
            pl.BlockSpec((S, hb * dn), lambda b, h, i: (b, h)),
            pl.BlockSpec((S, hb * dv), lambda b, h, i: (b, H // hb + h)),
            pl.BlockSpec((S, dr), lambda b, h, i: (b, 0)),
        ],
        out_specs=pl.BlockSpec((tq, hb * dv), lambda b, h, i: (b * nq + i, h)),
        out_shape=jax.ShapeDtypeStruct((B * S, H * dv), bf16),
        scratch_shapes=[pltpu.VMEM((hb, S, 2 * LANES), bf16)],
        compiler_params=_params("parallel", "parallel", "arbitrary"), name="attn_prompt",
    )(qn, qr_hm, kv, kv, krb)


def _attn_sample_kernel(pt_ref, ql_ref, qr_ref, cn_ref, krn_ref, lat_hbm, krt_hbm, o_ref,
                        latbuf, krtbuf, lat_sem, krt_sem, cb_ref, kt_ref, *, n_seq, n_pages, G, P, H, sub):
    b = pl.program_id(0)
    ql = ql_ref[0]
    qr = qr_ref[0]
    R = ql.shape[0]
    pg = latbuf.shape[2]

    def group_copies(seq, g):
        slot = g % 2
        out = []
        for k in range(P):
            page = pt_ref[seq * n_pages + g * P + k]
            out.append(pltpu.make_async_copy(lat_hbm.at[page], latbuf.at[slot, k], lat_sem.at[slot]))
            out.append(pltpu.make_async_copy(krt_hbm.at[page], krtbuf.at[slot, k], krt_sem.at[slot]))
        return out

    @pl.when(b == 0)
    def _():
        for cp in group_copies(0, 0):
            cp.start()

    cn = cn_ref[0].astype(bf16)
    S = cn.shape[0]
    s = (lax.dot_general(ql, cn, _NT, preferred_element_type=f32)
         + lax.dot_general(qr, krn_ref[0].astype(bf16), _NT, preferred_element_type=f32))
    qpos = lax.broadcasted_iota(jnp.int32, (R, S), 0) // H
    kpos = lax.broadcasted_iota(jnp.int32, (R, S), 1)
    s = jnp.where(kpos <= qpos, s, -jnp.inf)
    m = jnp.max(s, axis=-1, keepdims=True)
    p = jnp.exp(s - m)
    l = jnp.sum(p, axis=-1, keepdims=True)
    acc = jnp.dot(p.astype(bf16), cn, preferred_element_type=f32)

    nsb = P // sub
    w = sub * pg
    for g in range(G):
        slot = g % 2
        if g + 1 < G:
            for cp in group_copies(b, g + 1):
                cp.start()
        else:
            @pl.when(b + 1 < n_seq)
            def _():
                for cp in group_copies(b + 1, 0):
                    cp.start()
        for cp in group_copies(b, g):
            cp.wait()
        ss = []
        for sb in range(nsb):
            for k in range(sb * sub, (sb + 1) * sub):
                cb_ref[k * pg:(k + 1) * pg, :] = latbuf[slot, k].astype(bf16)
                kt_ref[:, k * pg:(k + 1) * pg] = krtbuf[slot, k].astype(bf16)
            ss.append(lax.dot_general(ql, cb_ref[sb * w:(sb + 1) * w, :], _NT, preferred_element_type=f32)
                      + jnp.dot(qr, kt_ref[:, sb * w:(sb + 1) * w], preferred_element_type=f32))
        m_new = m
        for s in ss:
            m_new = jnp.maximum(m_new, jnp.max(s, axis=-1, keepdims=True))
        alpha = jnp.exp(m - m_new)
        m = m_new
        l = l * alpha
        acc = acc * alpha
        for sb, s in enumerate(ss):
            p = jnp.exp(s - m_new)
            l = l + jnp.sum(p, axis=-1, keepdims=True)
            acc = acc + jnp.dot(p.astype(bf16), cb_ref[sb * w:(sb + 1) * w, :], preferred_element_type=f32)
    o_ref[0] = (acc / l).astype(o_ref.dtype)


def attn_sample(page_table, ql, qr, c_new, kr_new, cache_lat, cache_krt, *, H, pages_per_group):
    B, R, C = ql.shape
    dr = qr.shape[2]
    S = c_new.shape[1]
    n_pages = page_table.shape[1]
    P = pages_per_group
    G = n_pages // P
    assert G * P == n_pages and G % 2 == 0, "page groups alternate between two buffer slots"
    pg = cache_lat.shape[1]
    seq = lambda b, pt_ref: (b, 0, 0)
    kern = functools.partial(_attn_sample_kernel, n_seq=B, n_pages=n_pages, G=G, P=P, H=H,
                             sub=min(P, SAMPLE_SUB_PAGES))
    return pl.pallas_call(
        kern,
        grid_spec=pltpu.PrefetchScalarGridSpec(
            num_scalar_prefetch=1, grid=(B,),
            in_specs=[pl.BlockSpec((1, R, C), seq), pl.BlockSpec((1, R, dr), seq),
                      pl.BlockSpec((1, S, C), seq), pl.BlockSpec((1, S, dr), seq),
                      pl.BlockSpec(memory_space=pl.ANY), pl.BlockSpec(memory_space=pl.ANY)],
            out_specs=pl.BlockSpec((1, R, C), seq),
            scratch_shapes=[pltpu.VMEM((2, P, pg, C), cache_lat.dtype), pltpu.VMEM((2, P, dr, pg), cache_krt.dtype),
                            pltpu.SemaphoreType.DMA((2,)), pltpu.SemaphoreType.DMA((2,)),
                            pltpu.VMEM((P * pg, C), bf16), pltpu.VMEM((dr, P * pg), bf16)]),
        out_shape=jax.ShapeDtypeStruct((B, R, C), bf16),
        compiler_params=_params("arbitrary"), name="attn_sample",
    )(page_table.reshape(-1), ql, qr, c_new, kr_new, cache_lat, cache_krt)


def _rope_tables(pos, half):
    inv = ROPE_THETA ** (-jnp.arange(half, dtype=f32) / half)
    ang = pos[:, None] * inv[None, :]
    cos, sin = jnp.cos(ang), jnp.sin(ang)
    return jnp.concatenate([cos, cos], axis=1), jnp.concatenate([-sin, sin], axis=1)


def _swap_halves(w, r):
    k, n = w.shape
    return w.reshape(k, n // r, 2, r // 2)[:, :, ::-1, :].reshape(k, n)


def kernel(x_prompt, x_sample, cache_kv_latent, cache_k_rope, state_ssm, state_conv, page_table, ffn_pre_g, ffn_post_g, ffn_w_gate, ffn_w_up, ffn_w_down, mix_pre_g, mix_post_g, m_in_proj, m_conv_w, m_conv_b, m_dt_bias, m_A_log, m_D, m_norm_g, m_out_proj, kv_norm_g, w_dkv, kv_latent_norm_g, w_uk, w_uv, q_w_dq, q_norm_g, q_w_uq, attn_w_o):
    Bp, S, D = x_prompt.shape
    Bs, Ls, _ = x_sample.shape
    Tp, Ts = Bp * S, Bs * Ls
    T = Tp + Ts
    depth = ffn_pre_g.shape[0]
    n_a = state_ssm.shape[0]
    assert depth == 2 and n_a == 1, "layer pattern: one SSD layer then one MLA layer"
    H, P, N = state_ssm.shape[2:]
    G = SSM_GROUPS
    di = H * P
    conv_dim = state_conv.shape[3]
    C = cache_kv_latent.shape[2]
    dr = cache_k_rope.shape[2]
    Hm, dn = w_uk.shape[1:]
    dv = w_uv.shape[2]
    past_len = page_table.shape[1] * cache_kv_latent.shape[1]
    scale = float(dn + dr) ** -0.5
    tm = 512
    row = lambda v: v.reshape(1, -1)

    wg_all, wu_all, wd_all = ffn_w_gate.astype(bf16), ffn_w_up.astype(bf16), ffn_w_down.astype(bf16)

    def ffn(h, layer, j, **kw):
        tiles = dict(tm=tm, tf=512) if kw else dict(tm=2 * tm, tf=256)
        return ffn_half(h, row(ffn_pre_g[layer, j]), wg_all, wu_all, wd_all, row(ffn_post_g[layer, j]),
                        sel=(layer, j), **tiles, **kw)

    pos = jnp.concatenate([jnp.tile(jnp.arange(S, dtype=f32), Bp),
                           jnp.tile(past_len + jnp.arange(Ls, dtype=f32), Bs)])
    cos64, sin64 = _rope_tables(pos, dr // 2)
    cos128, sin128 = jnp.tile(cos64, (1, 2)), jnp.tile(sin64, (1, 2))

    h = ffn(x_prompt.reshape(Tp, D), 0, 0, hb=x_sample.reshape(Ts, D))
    w_in = m_in_proj[0]
    nzx = di + conv_dim
    w_zx = w_in[:, :nzx].astype(bf16)
    w_dt = w_in[:, nzx:]
    w_dtp = jnp.pad(w_dt, ((0, 0), (0, LANES - H))).astype(bf16)
    w_dtx = jnp.repeat(w_dt, P, axis=1).astype(bf16)
    g_mix0 = row(mix_pre_g[0])
    zx = fused_matmul(h, w_zx, tm=2 * tm, tn=1024, g_pre=g_mix0, name="in_proj_zx")
    dtp = fused_matmul(h, w_dtp, tm=tm, g_pre=g_mix0, n_rows=Tp, name="in_proj_dt")
    dtx = fused_matmul(h, w_dtx, tm=tm, tn=1024, g_pre=g_mix0, n_rows=Ts, row_block_offset=Tp // tm,
                       name="in_proj_dtx")
    cw, cb = m_conv_w[0], row(m_conv_b[0])
    dx = row(jnp.repeat(m_D[0], P))
    ng = row(m_norm_g[0])
    pad_h = lambda v: row(jnp.pad(v, (0, LANES - H)))
    yg_p, ssm_p = ssd_prompt(zx, dtp, cw, cb, pad_h(m_dt_bias[0]), pad_h(m_A_log[0]), dx, ng,
                             B=Bp, S=S, H=H, P=P, N=N, G=G)
    yg_s, ssm_s = ssd_sample(zx, dtx, state_conv[0], state_ssm[0].reshape(Bs, G, di // G, N), cw, cb,
                             row(jnp.repeat(m_dt_bias[0], P)), row(jnp.repeat(m_A_log[0], P)), dx, ng,
                             B=Bs, L=Ls, row0=Tp, G=G, N=N)
    h = fused_matmul(yg_p, m_out_proj[0].astype(bf16), xb=yg_s.astype(bf16), tm=tm, tn=512, mode="resnorm", res=h,
                     g_post=row(mix_post_g[0]), name="out_proj")
    h = ffn(h, 0, 1)

    tail = CONV_W - 1
    conv_prompt = jnp.stack([lax.slice(zx, ((b + 1) * S - tail, di), ((b + 1) * S, nzx)) for b in range(Bp)])[None]
    conv_sample = zx[Tp:].reshape(Bs, Ls, -1)[:, Ls - tail:, di:][None]
    ssm_prompt = ssm_p.reshape(1, Bp, H, P, N)
    ssm_sample = ssm_s.reshape(1, Bs, H, P, N)

    w_kv = jnp.concatenate([w_dkv, _swap_halves(w_dkv[:, C:], dr)], axis=1).astype(bf16)
    lat, kr, krb = fused_matmul(h, w_kv, tm=tm, mode="kv", g_pre=row(kv_norm_g), g_post=row(kv_latent_norm_g),
                                cos=cos64, sin=sin64, n_lat=C, name="shared_kv")

    h = ffn(h, 1, 0)
    cq = fused_matmul(h, q_w_dq[0].astype(bf16), tm=tm, mode="norm", g_pre=row(mix_pre_g[1]),
                      g_post=row(q_norm_g[0]), out_dtype=bf16, name="q_down")
    w_uq = q_w_uq[0]
    qrank = w_uq.shape[0]
    w_qn = w_uq[:, :, :dn].reshape(qrank, Hm * dn).astype(bf16)
    w_qr = w_uq[:, :, dn:].reshape(qrank, Hm * dr)
    w_qr2 = jnp.concatenate([w_qr, _swap_halves(w_qr, dr)], axis=1).astype(bf16)
    qn = fused_matmul(cq, w_qn, tm=tm, scale=scale, out_dtype=bf16, name="q_nope")
    qr = fused_matmul(cq, w_qr2, tm=tm, mode="rope", cos=cos128, sin=sin128, scale=scale, out_dtype=bf16,
                      name="q_rope")

    w_kvup = jnp.concatenate([w_uk.reshape(C, Hm * dn), w_uv.reshape(C, Hm * dv)], axis=1).astype(bf16)
    kv = fused_matmul(lat, w_kvup, tm=tm, tn=1024, n_rows=Tp, out_dtype=bf16, name="kv_up")
    qr_hm = qr.reshape(T, Hm, dr).transpose(1, 0, 2)
    o_p = attn_prompt(qn, qr_hm, kv, krb, B=Bp, S=S, H=Hm, dn=dn, dv=dv, tq=512, hb=2)

    w_ukh = w_uk.transpose(1, 2, 0).astype(bf16)
    w_uvh = w_uv.transpose(1, 0, 2).astype(bf16)
    ql = head_matmul(qn[Tp:], w_ukh, name="q_absorb")
    o_lat = attn_sample(page_table, ql.reshape(Bs, Ls * Hm, C), qr[Tp:].reshape(Bs, Ls * Hm, dr),
                        lat[Tp:].reshape(Bs, Ls, C), kr[Tp:].reshape(Bs, Ls, dr),
                        cache_kv_latent, cache_k_rope.transpose(0, 2, 1), H=Hm,
                        pages_per_group=min(32, page_table.shape[1] // 2))
    o_s = head_matmul(o_lat.reshape(Ts, Hm * C), w_uvh, name="v_up")

    h = fused_matmul(o_p, attn_w_o[0].reshape(Hm * dv, D).astype(bf16), xb=o_s, tm=tm, tn=512, mode="resnorm",
                     res=h, g_post=row(mix_post_g[1]), name="attn_out")
    y_p, y_s = ffn(h, 1, 1, split_out=Tp)

    y_prompt = y_p.reshape(Bp, S, D)
    y_sample = y_s.reshape(Bs, Ls, D)
    return (y_prompt, y_sample, lat[:Tp].reshape(Bp, S, C), kr[:Tp].reshape(Bp, S, dr), ssm_prompt, conv_prompt,
            lat[Tp:].reshape(Bs, Ls, C), kr[Tp:].reshape(Bs, Ls, dr), ssm_sample, conv_sample)
```
